```python
import jax, jax.numpy as jnp
from jax import lax
import numpy as np

D_MODEL = 4096
BATCH = 2
SEQ = 8192
DEPTH = 1

CHUNK = 64
QUERY_BLOCK = 128
MIX_WIDTH = D_MODEL
FOX_WIDTH = MIX_WIDTH // 2
RWKV_WIDTH = MIX_WIDTH - FOX_WIDTH
FOX_HEAD_DIM = 128
FOX_HEADS = FOX_WIDTH // FOX_HEAD_DIM
RWKV_HEAD_DIM = 64
RWKV_HEADS = RWKV_WIDTH // RWKV_HEAD_DIM
DECAY_LORA = max(32, int(round(1.8 * RWKV_WIDTH ** 0.5 / 32)) * 32)
AAA_LORA = max(32, int(round(1.8 * RWKV_WIDTH ** 0.5 / 32)) * 32)
SHIFT_COLS = 3 * RWKV_WIDTH + DECAY_LORA + AAA_LORA
IN_SIZES = (FOX_WIDTH, FOX_WIDTH, FOX_WIDTH, FOX_HEADS,
            RWKV_WIDTH, RWKV_WIDTH, RWKV_WIDTH, DECAY_LORA, AAA_LORA,
            MIX_WIDTH)
IN_COLS = sum(IN_SIZES)
ALPHA = (2 * DEPTH) ** 0.25
BETA = (8 * DEPTH) ** -0.25
LN_EPS = 1e-5
GN_EPS = 64e-5
FOX_SCALE = FOX_HEAD_DIM ** -0.5

kernel_name = "fox_rwkv7_hybrid_deepnorm_layer"


def _split_points(sizes):
    pts, acc = [], 0
    for s in sizes[:-1]:
        acc += s
        pts.append(acc)
    return pts


def _layernorm(x, gain, bias, eps):
    xf = x.astype(jnp.float32)
    mu = jnp.mean(xf, axis=-1, keepdims=True)
    var = jnp.mean(jnp.square(xf - mu), axis=-1, keepdims=True)
    return (xf - mu) * lax.rsqrt(var + eps) * gain.astype(jnp.float32) + bias.astype(jnp.float32)


def _fox_attention(q, k, v, log_f):
    B, S, H, Dh = q.shape
    qf = jnp.transpose(q, (0, 2, 1, 3)).astype(jnp.float32) * FOX_SCALE
    kf = jnp.transpose(k, (0, 2, 1, 3)).astype(jnp.float32)
    vf = jnp.transpose(v, (0, 2, 1, 3)).astype(jnp.float32)
    c = jnp.transpose(jnp.cumsum(log_f.astype(jnp.float32), axis=1), (0, 2, 1))
    key_pos = jnp.arange(S)

    def block(i):
        start = i * QUERY_BLOCK
        qb = lax.dynamic_slice_in_dim(qf, start, QUERY_BLOCK, axis=2)
        cb = lax.dynamic_slice_in_dim(c, start, QUERY_BLOCK, axis=2)
        q_pos = start + jnp.arange(QUERY_BLOCK)
        logits = jnp.einsum('bhqd,bhkd->bhqk', qb, kf) + cb[..., :, None] - c[..., None, :]
        logits = jnp.where(key_pos[None, :] <= q_pos[:, None], logits, -jnp.inf)
        p = jax.nn.softmax(logits, axis=-1)
        return jnp.einsum('bhqk,bhkd->bhqd', p, vf)

    out = lax.map(block, jnp.arange(S // QUERY_BLOCK))
    return jnp.transpose(out, (1, 0, 3, 2, 4)).reshape(B, S, H * Dh)


def _rwkv7_scan(r, decay, k, v, a_vec, b_vec):
    B, S, H, N = r.shape
    n_chunks = S // CHUNK

    def to_chunks(t):
        return jnp.transpose(t, (1, 0, 2, 3)).reshape(n_chunks, CHUNK, B, H, N)

    xs = tuple(to_chunks(t) for t in (r, decay, k, v, a_vec, b_vec))

    def frame_step(state, inp):
        r_t, w_t, k_t, v_t, a_t, b_t = inp
        sa = jnp.einsum('bhvk,bhk->bhv', state, a_t)
        state = (state * w_t[:, :, None, :] + sa[..., None] * b_t[:, :, None, :]
                 + v_t[..., None] * k_t[:, :, None, :])
        return state, jnp.einsum('bhvk,bhk->bhv', state, r_t)

    def chunk_step(state, chunk_inp):
        return lax.scan(frame_step, state, chunk_inp)

    state0 = jnp.zeros((B, H, N, N), jnp.float32)
    _, y = lax.scan(chunk_step, state0, xs)
    return jnp.transpose(y.reshape(S, B, H, N), (1, 0, 2, 3))


def setup_inputs(seed: int = 0) -> dict:
    key = jax.random.key(seed)
    ks = jax.random.split(key, 20)
    f32 = jnp.float32
    x = jax.random.normal(ks[0], (BATCH, SEQ, D_MODEL), f32)
    col_scale = jnp.concatenate([
        jnp.ones((2 * FOX_WIDTH,), f32), jnp.full((FOX_WIDTH,), BETA, f32), jnp.ones((FOX_HEADS,), f32),
        jnp.ones((2 * RWKV_WIDTH,), f32), jnp.full((RWKV_WIDTH,), BETA, f32),
        jnp.ones((DECAY_LORA + AAA_LORA + MIX_WIDTH,), f32)])
    w_in = jax.random.normal(ks[1], (D_MODEL, IN_COLS), f32) * (D_MODEL ** -0.5) * col_scale
    f_bias = 4.0 + 0.5 * jax.random.normal(ks[2], (FOX_HEADS,), f32)
    mu_shift = jax.random.uniform(ks[3], (SHIFT_COLS,), f32)
    w0 = 0.5 * jax.random.normal(ks[4], (RWKV_WIDTH,), f32)
    w_up = 0.5 * jax.random.normal(ks[5], (DECAY_LORA, RWKV_WIDTH), f32) * DECAY_LORA ** -0.5
    a0 = 0.1 * jax.random.normal(ks[6], (RWKV_WIDTH,), f32)
    a_up = 0.5 * jax.random.normal(ks[7], (AAA_LORA, RWKV_WIDTH), f32) * AAA_LORA ** -0.5
    k_k = 0.85 + 0.1 * jax.random.normal(ks[8], (RWKV_WIDTH,), f32)
    k_a = 1.0 + 0.1 * jax.random.normal(ks[9], (RWKV_WIDTH,), f32)
    r_k = 0.1 * jax.random.normal(ks[10], (RWKV_HEADS, RWKV_HEAD_DIM), f32)
    gn_gain = 1.0 + 0.1 * jax.random.normal(ks[11], (RWKV_WIDTH,), f32)
    gn_bias = 0.01 * jax.random.normal(ks[12], (RWKV_WIDTH,), f32)
    w_out = jax.random.normal(ks[13], (MIX_WIDTH, D_MODEL), f32) * (MIX_WIDTH ** -0.5) * BETA
    ln_gain = 1.0 + 0.1 * jax.random.normal(ks[14], (D_MODEL,), f32)
    ln_bias = 0.01 * jax.random.normal(ks[15], (D_MODEL,), f32)
    return {"x": x, "w_in": w_in, "f_bias": f_bias, "mu_shift": mu_shift, "w0": w0, "w_up": w_up,
            "a0": a0, "a_up": a_up, "k_k": k_k, "k_a": k_a, "r_k": r_k, "gn_gain": gn_gain,
            "gn_bias": gn_bias, "w_out": w_out, "ln_gain": ln_gain, "ln_bias": ln_bias}


def reference(x, w_in, f_bias, mu_shift, w0, w_up, a0, a_up, k_k, k_a, r_k, gn_gain, gn_bias,
              w_out, ln_gain, ln_bias):
    B, S, _ = x.shape
    f32 = jnp.float32
    for _layer in range(DEPTH):
        proj = jnp.einsum('bsd,dc->bsc', x, w_in)
        fox_end = 3 * FOX_WIDTH + FOX_HEADS
        fox_cols = proj[..., :fox_end]
        rw_cols = proj[..., fox_end:fox_end + SHIFT_COLS]
        z = proj[..., fox_end + SHIFT_COLS:]

        fq, fk, fv, ff = jnp.split(fox_cols, _split_points(IN_SIZES[:4]), axis=-1)
        log_f = jax.nn.log_sigmoid(ff.astype(f32) + f_bias.astype(f32))
        fox_out = _fox_attention(fq.reshape(B, S, FOX_HEADS, FOX_HEAD_DIM),
                                 fk.reshape(B, S, FOX_HEADS, FOX_HEAD_DIM),
                                 fv.reshape(B, S, FOX_HEADS, FOX_HEAD_DIM), log_f)

        rw = rw_cols.astype(f32)
        rw_prev = jnp.pad(rw, ((0, 0), (1, 0), (0, 0)))[:, :-1]
        rw = rw + (rw_prev - rw) * mu_shift.astype(f32)
        r, kr, vr, wd, ad = jnp.split(rw, _split_points(IN_SIZES[4:9]), axis=-1)
        w_raw = -jax.nn.softplus(-(w0 + jnp.tanh(wd) @ w_up)) - 0.5
        decay = jnp.exp(-jnp.exp(w_raw))
        a = jax.nn.sigmoid(a0 + ad @ a_up)
        hs = (B, S, RWKV_HEADS, RWKV_HEAD_DIM)
        kk = (kr * k_k).reshape(hs)
        kk = kk / jnp.maximum(jnp.linalg.norm(kk, axis=-1, keepdims=True), 1e-12)
        kr = kr * (1.0 + (a - 1.0) * k_a)
        rh, kh, vh, ah = r.reshape(hs), kr.reshape(hs), vr.reshape(hs), a.reshape(hs)
        y = _rwkv7_scan(rh, decay.reshape(hs), kh, vh, -kk, kk * ah)
        mu = jnp.mean(y, axis=-1, keepdims=True)
        var = jnp.mean(jnp.square(y - mu), axis=-1, keepdims=True)
        y = ((y - mu) * lax.rsqrt(var + GN_EPS)).reshape(B, S, RWKV_WIDTH) * gn_gain + gn_bias
        bonus = jnp.sum(rh * kh * r_k, axis=-1, keepdims=True) * vh
        rwkv_out = y + bonus.reshape(B, S, RWKV_WIDTH)

        h = jnp.concatenate([fox_out, rwkv_out], axis=-1) * jax.nn.silu(z.astype(f32))
        out = jnp.einsum('bsc,cd->bsd', h.astype(x.dtype), w_out)
        x = _layernorm(ALPHA * x.astype(f32) + out.astype(f32), ln_gain, ln_bias, LN_EPS).astype(x.dtype)
    return x
```

```python
import functools

import jax
import jax.numpy as jnp
from jax import lax
from jax.experimental import pallas as pl
from jax.experimental.pallas import tpu as pltpu

F32 = jnp.float32
BF16 = jnp.bfloat16

LANES = 128
FOX_HEAD_DIM = 128
RWKV_HEAD_DIM = 64
RWKV_CHUNK = 64
LN_EPS = 1e-5
GN_EPS = 64e-5
VMEM_LIMIT_BYTES = 56 * 1024 * 1024


def _bdot(a, b, dims=(((1,), (0,)), ((), ()))):
    return lax.dot_general(a.astype(BF16), b.astype(BF16), dims, preferred_element_type=F32)


def _split3_dot(m_bf16, x):
    hi = x.astype(BF16)
    r1 = x - hi.astype(F32)
    mid = r1.astype(BF16)
    lo = (r1 - mid.astype(F32)).astype(BF16)
    dot = functools.partial(jnp.dot, preferred_element_type=F32)
    return dot(m_bf16, hi) + dot(m_bf16, mid) + dot(m_bf16, lo)


def _split3_dot_right(x, m_bf16):
    hi = x.astype(BF16)
    r1 = x - hi.astype(F32)
    mid = r1.astype(BF16)
    lo = (r1 - mid.astype(F32)).astype(BF16)
    dot = functools.partial(jnp.dot, preferred_element_type=F32)
    return dot(hi, m_bf16) + dot(mid, m_bf16) + dot(lo, m_bf16)


def _pick_tile(n, preferred):
    t = min(preferred, n) // LANES * LANES
    while n % t:
        t -= LANES
    return t


def _mm_kernel(a_ref, b_ref, o_ref):
    o_ref[...] = jnp.dot(a_ref[...], b_ref[...], preferred_element_type=F32).astype(o_ref.dtype)


def _mm_scaled_kernel(a_ref, b_ref, s_ref, o_ref):
    acc = jnp.dot(a_ref[...], b_ref[...], preferred_element_type=F32)
    o_ref[...] = (acc * s_ref[...]).astype(o_ref.dtype)


def _matmul(a, b, out_dtype, tm, tn, col_scale=None):
    m, k = a.shape
    n = b.shape[1]
    tm, tn = _pick_tile(m, tm), _pick_tile(n, tn)
    in_specs = [pl.BlockSpec((tm, k), lambda i, j: (i, 0)),
                pl.BlockSpec((k, tn), lambda i, j: (0, j))]
    args = [a, b]
    body = _mm_kernel
    if col_scale is not None:
        in_specs.append(pl.BlockSpec((1, tn), lambda i, j: (0, j)))
        args.append(col_scale)
        body = _mm_scaled_kernel
    return pl.pallas_call(
        body,
        grid=(m // tm, n // tn),
        in_specs=in_specs,
        out_specs=pl.BlockSpec((tm, tn), lambda i, j: (i, j)),
        out_shape=jax.ShapeDtypeStruct((m, n), out_dtype),
        name="in_proj_scaled" if col_scale is not None else "in_proj",
        compiler_params=pltpu.CompilerParams(
            dimension_semantics=("parallel", "arbitrary"), vmem_limit_bytes=VMEM_LIMIT_BYTES),
    )(*args)


def _gate_kernel(x_ref, b_ref, o_ref, carry_ref):
    @pl.when(pl.program_id(1) == 0)
    def _():
        carry_ref[...] = jnp.zeros_like(carry_ref)

    t = x_ref.shape[1]
    x = x_ref[0] + b_ref[...]
    logf = jnp.minimum(x, 0.0) - jnp.log1p(jnp.exp(-jnp.abs(x)))
    tri = (lax.broadcasted_iota(jnp.int32, (t, t), 1)
           <= lax.broadcasted_iota(jnp.int32, (t, t), 0)).astype(BF16)
    c = _split3_dot(tri, logf) + carry_ref[0:1, :]
    o_ref[0] = c
    carry_ref[0:1, :] = c[t - 1:t, :]


def _gate_cumsum(lora3, bias_row, col_block, t):
    b, s, _ = lora3.shape
    t = min(t, s)
    return pl.pallas_call(
        _gate_kernel,
        grid=(b, s // t),
        in_specs=[pl.BlockSpec((1, t, LANES), lambda bi, i: (bi, i, col_block)),
                  pl.BlockSpec((1, LANES), lambda bi, i: (0, 0))],
        out_specs=pl.BlockSpec((1, t, LANES), lambda bi, i: (bi, i, 0)),
        out_shape=jax.ShapeDtypeStruct((b, s, LANES), F32),
        name="fox_gate_cumsum",
        scratch_shapes=[pltpu.VMEM((8, LANES), F32)],
        compiler_params=pltpu.CompilerParams(dimension_semantics=("parallel", "arbitrary")),
    )(lora3, bias_row)


def _fox_kernel(q_ref, k_ref, v_ref, c_ref, z_ref, o_ref, *, tq):
    i = pl.program_id(2)
    q = q_ref[0]

    def block(j, carry, diagonal):
        m, l, acc = carry
        start = pl.multiple_of(j * tq, tq)
        k = k_ref[0, pl.ds(start, tq), :]
        v = v_ref[0, pl.ds(start, tq), :]
        s = lax.dot_general(q, k, (((1,), (1,)), ((), ())), preferred_element_type=F32)
        s = s - c_ref[0, 0, j]
        if diagonal:
            row = lax.broadcasted_iota(jnp.int32, (tq, tq), 0)
            col = lax.broadcasted_iota(jnp.int32, (tq, tq), 1)
            s = jnp.where(col <= row, s, -jnp.inf)
        m_new = jnp.maximum(m, jnp.max(s, axis=-1, keepdims=True))
        p = jnp.exp(s - m_new)
        alpha = jnp.exp(m - m_new)
        l = alpha * l + jnp.sum(p, axis=-1, keepdims=True)
        acc = alpha * acc + jnp.dot(p.astype(BF16), v, preferred_element_type=F32)
        return m_new, l, acc

    init = (jnp.full((tq, 1), -jnp.inf, F32), jnp.zeros((tq, 1), F32),
            jnp.zeros((tq, FOX_HEAD_DIM), F32))
    carry = lax.fori_loop(0, i, lambda j, c: block(j, c, False), init)
    _, l, acc = block(i, carry, True)
    z = z_ref[0].astype(F32)
    gate = z / (1.0 + jnp.exp(-z))
    o_ref[0] = (acc / l * gate).astype(o_ref.dtype)


def _fox_attention(qkvz, c5, n_heads, tq):
    b, s, _ = qkvz.shape
    nq = s // tq
    h = n_heads
    return pl.pallas_call(
        functools.partial(_fox_kernel, tq=tq),
        grid=(b, h, nq),
        in_specs=[pl.BlockSpec((1, tq, FOX_HEAD_DIM), lambda bi, hi, i: (bi, i, hi)),
                  pl.BlockSpec((1, s, FOX_HEAD_DIM), lambda bi, hi, i: (bi, 0, h + hi)),
                  pl.BlockSpec((1, s, FOX_HEAD_DIM), lambda bi, hi, i: (bi, 0, 2 * h + hi)),
                  pl.BlockSpec((1, 1, nq, 1, tq), lambda bi, hi, i: (bi, hi, 0, 0, 0)),
                  pl.BlockSpec((1, tq, FOX_HEAD_DIM), lambda bi, hi, i: (bi, i, 3 * h + hi))],
        out_specs=pl.BlockSpec((1, tq, FOX_HEAD_DIM), lambda bi, hi, i: (bi, i, hi)),
        out_shape=jax.ShapeDtypeStruct((b, s, h * FOX_HEAD_DIM), BF16),
        name="fox_attention",
        compiler_params=pltpu.CompilerParams(
            dimension_semantics=("parallel", "parallel", "arbitrary"), vmem_limit_bytes=VMEM_LIMIT_BYTES),
    )(qkvz, qkvz, qkvz, c5, qkvz)


def _stack_heads(x, head0):
    zero = jnp.zeros_like(x)
    return jnp.concatenate([jnp.where(head0, x, zero), jnp.where(head0, zero, x)], axis=0)


def _rwkv_chunk(r, k2, v, av, bv, logw, sbd, masks):
    head0, strict, incl, tri, blockdiag, eye = masks
    c = r.shape[0]
    lg = _split3_dot(tri, logw)
    lgc = lg[c - 1:c, :]
    e_pos = jnp.exp(lg)
    e_neg = jnp.exp(-lg)
    e_end = jnp.exp(lgc - lg)
    rt = r * e_pos
    at = av * jnp.exp(lg - logw)
    bt = bv * e_neg
    kt = k2 * e_neg
    bh = bv * e_end
    kh = k2 * e_end
    gc = jnp.exp(lgc)

    nt = (((1,), (1,)), ((), ()))
    big = _bdot(jnp.concatenate([at, rt], axis=0),
                jnp.concatenate([_stack_heads(bt, head0), _stack_heads(kt, head0)], axis=0), nt)
    zero = jnp.zeros((c, LANES), F32)
    l_mat = jnp.where(strict, big[:c, :LANES], zero)
    a_ak = jnp.where(strict, big[:c, LANES:], zero)
    a_rb = jnp.where(incl, big[c:, :LANES], zero)
    a_rk = jnp.where(incl, big[c:, LANES:], zero)

    sv = _stack_heads(v, head0)
    x = jnp.concatenate([at, _bdot(a_ak, sv)], axis=1)
    lp = l_mat
    n_steps = max(1, (c - 1).bit_length())
    for step in range(n_steps):
        sx = jnp.concatenate([_stack_heads(x[:, :LANES], head0), _stack_heads(x[:, LANES:], head0)], axis=1)
        x = x + _bdot(lp, sx)
        if step + 1 < n_steps:
            lp = _bdot(lp, _stack_heads(lp, head0))
    p_mat = x[:, :LANES]
    u_v = x[:, LANES:]

    rhs = jnp.concatenate(
        [jnp.concatenate([_stack_heads(p_mat, head0), _stack_heads(u_v, head0)], axis=1),
         jnp.concatenate([jnp.zeros((2 * c, LANES), F32), sv], axis=1)], axis=0)
    qy = _bdot(jnp.concatenate([a_rb, a_rk], axis=1), rhs)
    q_mat = rt + qy[:, :LANES]
    y_v = qy[:, LANES:]

    tn = (((0,), (0,)), ((), ()))
    gh = _bdot(jnp.concatenate([bh, kh], axis=0),
               jnp.concatenate([jnp.concatenate([p_mat, u_v], axis=1),
                                jnp.concatenate([zero, v], axis=1)], axis=0), tn)
    g_mat = jnp.where(blockdiag, gh[:, :LANES], 0.0) + eye * gc
    h_mat = jnp.where(blockdiag, gh[:, LANES:], 0.0)

    y = _bdot(q_mat, sbd) + y_v
    sbd_new = _bdot(g_mat, sbd) + h_mat
    return y, sbd_new


def _rwkv_kernel(r_ref, k_ref, v_ref, lo_ref, z_ref, mur_ref, muk_ref, muv_ref, mulo_ref,
                 wup_ref, aup_ref, w0_ref, a0_ref, kk_ref, ka_ref, rk_ref, gg_ref, gb_ref,
                 o_ref, prev_ref, sbd_ref):
    t = r_ref.shape[1]
    c = RWKV_CHUNK
    n = RWKV_HEAD_DIM

    @pl.when(pl.program_id(2) == 0)
    def _():
        prev_ref[...] = jnp.zeros_like(prev_ref)
        sbd_ref[...] = jnp.zeros_like(sbd_ref)

    first_row = lax.broadcasted_iota(jnp.int32, (t, 1), 0) == 0

    def token_shift(x, mu, lo, hi):
        prev = jnp.where(first_row, prev_ref[0:1, lo:hi], pltpu.roll(x, 1, axis=0))
        prev_ref[0:1, lo:hi] = x[t - 1:t, :]
        return x + (prev - x) * mu

    r = token_shift(r_ref[0], mur_ref[...], 0, LANES)
    k = token_shift(k_ref[0], muk_ref[...], LANES, 2 * LANES)
    v = token_shift(v_ref[0], muv_ref[...], 2 * LANES, 3 * LANES)
    lora = token_shift(lo_ref[0], mulo_ref[...], 3 * LANES, 3 * LANES + lo_ref.shape[2])

    dec = _bdot(jnp.tanh(lora), wup_ref[...])
    wx = -(w0_ref[...] + dec)
    w_raw = -(jnp.maximum(wx, 0.0) + jnp.log1p(jnp.exp(-jnp.abs(wx)))) - 0.5
    logw = -jnp.exp(w_raw)
    a = 1.0 / (1.0 + jnp.exp(-(a0_ref[...] + _bdot(lora, aup_ref[...]))))

    lane_r = lax.broadcasted_iota(jnp.int32, (LANES, LANES), 0)
    lane_c = lax.broadcasted_iota(jnp.int32, (LANES, LANES), 1)
    blockdiag = (lane_r < n) == (lane_c < n)
    head_ones = blockdiag.astype(BF16)

    def head_sum(x):
        return _split3_dot_right(x, head_ones)

    kk = k * kk_ref[...]
    kk = kk / jnp.maximum(jnp.sqrt(head_sum(kk * kk)), 1e-12)
    k2 = k * (1.0 + (a - 1.0) * ka_ref[...])
    av = -kk
    bv = kk * a

    lane = lax.broadcasted_iota(jnp.int32, (c, LANES), 1)
    row = lax.broadcasted_iota(jnp.int32, (c, LANES), 0)
    head0 = lane < n
    col = jnp.where(head0, lane, lane - n)
    tri = (lax.broadcasted_iota(jnp.int32, (c, c), 1) <= lax.broadcasted_iota(jnp.int32, (c, c), 0)).astype(BF16)
    eye = (lane_r == lane_c).astype(F32)
    masks = (head0, col < row, col <= row, tri, blockdiag, eye)

    sbd = sbd_ref[...]
    ys = []
    for ci in range(t // c):
        sl = slice(ci * c, (ci + 1) * c)
        y, sbd = _rwkv_chunk(r[sl], k2[sl], v[sl], av[sl], bv[sl], logw[sl], sbd, masks)
        ys.append(y)
    sbd_ref[...] = sbd
    y = jnp.concatenate(ys, axis=0)

    mu = head_sum(y) * (1.0 / n)
    d = y - mu
    var = head_sum(d * d) * (1.0 / n)
    y = d * lax.rsqrt(var + GN_EPS) * gg_ref[...] + gb_ref[...]
    y = y + head_sum(r * k2 * rk_ref[...]) * v
    z = z_ref[0].astype(F32)
    o_ref[0] = (y * (z / (1.0 + jnp.exp(-z)))).astype(o_ref.dtype)


def _rwkv_mix(rkvl, qkvz, rows, wup_pad, aup_pad, n_pairs, lora_cols, z_block0, t):
    b, s, _ = rkvl.shape
    t = min(t, s)
    p = n_pairs
    lora_block = 3 * p * LANES // lora_cols
    assert lora_block * lora_cols == 3 * p * LANES
    tok = lambda off: pl.BlockSpec((1, t, LANES), lambda bi, j, i: (bi, i, off + j))
    prow = lambda off: pl.BlockSpec((1, LANES), lambda bi, j, i: (0, off + j))
    in_specs = [
        tok(0), tok(p), tok(2 * p),
        pl.BlockSpec((1, t, lora_cols), lambda bi, j, i: (bi, i, lora_block)),
        tok(z_block0),
        prow(0), prow(p), prow(2 * p),
        pl.BlockSpec((1, lora_cols), lambda bi, j, i: (0, lora_block)),
        pl.BlockSpec((lora_cols, LANES), lambda bi, j, i: (0, j)),
        pl.BlockSpec((lora_cols, LANES), lambda bi, j, i: (0, j)),
    ] + [prow(0)] * 7
    return pl.pallas_call(
        _rwkv_kernel,
        grid=(b, p, s // t),
        in_specs=in_specs,
        out_specs=pl.BlockSpec((1, t, LANES), lambda bi, j, i: (bi, i, j)),
        out_shape=jax.ShapeDtypeStruct((b, s, p * LANES), BF16),
        name="rwkv7_mix",
        scratch_shapes=[pltpu.VMEM((8, 3 * LANES + lora_cols), F32), pltpu.VMEM((LANES, LANES), F32)],
        compiler_params=pltpu.CompilerParams(
            dimension_semantics=("parallel", "parallel", "arbitrary"), vmem_limit_bytes=VMEM_LIMIT_BYTES),
    )(rkvl, rkvl, rkvl, rkvl, qkvz, rows["mu"], rows["mu"], rows["mu"], rows["mu"], wup_pad, aup_pad,
      rows["w0"], rows["a0"], rows["k_k"], rows["k_a"], rows["r_k"], rows["gn_gain"], rows["gn_bias"])


def _out_kernel(hf_ref, hr_ref, wt_ref, wb_ref, x_ref, g_ref, b_ref, o_ref, acc_ref, *, alpha, n_tiles):
    j = pl.program_id(1)
    out = (jnp.dot(hf_ref[...], wt_ref[...], preferred_element_type=F32)
           + jnp.dot(hr_ref[...], wb_ref[...], preferred_element_type=F32))
    acc_ref[j] = alpha * x_ref[...] + out

    @pl.when(j == n_tiles - 1)
    def _():
        tn = acc_ref.shape[2]
        d = n_tiles * tn
        total = acc_ref[0]
        for jj in range(1, n_tiles):
            total = total + acc_ref[jj]
        mean = jnp.sum(total, axis=-1, keepdims=True) * (1.0 / d)
        sq = jnp.square(acc_ref[0] - mean)
        for jj in range(1, n_tiles):
            sq = sq + jnp.square(acc_ref[jj] - mean)
        inv = lax.rsqrt(jnp.sum(sq, axis=-1, keepdims=True) * (1.0 / d) + LN_EPS)
        for jj in range(n_tiles):
            sl = slice(jj * tn, (jj + 1) * tn)
            o_ref[:, sl] = ((acc_ref[jj] - mean) * inv * g_ref[:, sl] + b_ref[:, sl]).astype(o_ref.dtype)


def _out_proj_layernorm(hf, hr, w_out_bf16, x2, ln_gain, ln_bias, alpha, tm, tn):
    m, d = x2.shape
    kf, kr = hf.shape[1], hr.shape[1]
    assert kf == kr and kf + kr == w_out_bf16.shape[0]
    tm, tn = min(tm, m), min(tn, d)
    n_tiles = d // tn
    return pl.pallas_call(
        functools.partial(_out_kernel, alpha=alpha, n_tiles=n_tiles),
        grid=(m // tm, n_tiles),
        in_specs=[pl.BlockSpec((tm, kf), lambda i, j: (i, 0)),
                  pl.BlockSpec((tm, kr), lambda i, j: (i, 0)),
                  pl.BlockSpec((kf, tn), lambda i, j: (0, j)),
                  pl.BlockSpec((kr, tn), lambda i, j: (1, j)),
                  pl.BlockSpec((tm, tn), lambda i, j: (i, j)),
                  pl.BlockSpec((1, d), lambda i, j: (0, 0)),
                  pl.BlockSpec((1, d), lambda i, j: (0, 0))],
        out_specs=pl.BlockSpec((tm, d), lambda i, j: (i, 0)),
        out_shape=jax.ShapeDtypeStruct((m, d), x2.dtype),
        name="out_proj_layernorm",
        scratch_shapes=[pltpu.VMEM((n_tiles, tm, tn), F32)],
        compiler_params=pltpu.CompilerParams(
            dimension_semantics=("parallel", "arbitrary"), vmem_limit_bytes=VMEM_LIMIT_BYTES),
    )(hf, hr, w_out_bf16, w_out_bf16, x2, ln_gain.reshape(1, d).astype(F32), ln_bias.reshape(1, d).astype(F32))


def kernel(x, w_in, f_bias, mu_shift, w0, w_up, a0, a_up, k_k, k_a, r_k, gn_gain, gn_bias, w_out, ln_gain, ln_bias):
    b, s, d = x.shape
    fox_heads = f_bias.shape[0]
    fw = fox_heads * FOX_HEAD_DIM
    rw = w0.shape[0]
    n_pairs = rw // LANES
    assert r_k.shape[1] == RWKV_HEAD_DIM and rw % LANES == 0
    dl, al = w_up.shape[0], a_up.shape[0]
    mix = fw + rw
    assert w_in.shape[1] == 3 * fw + fox_heads + 3 * rw + dl + al + mix
    depth = 1
    alpha = (2 * depth) ** 0.25

    o_ff = 3 * fw
    o_r = o_ff + fox_heads
    o_wd = o_r + 3 * rw
    o_ad = o_wd + dl
    o_z = o_ad + al

    lora_used = dl + al + fox_heads
    lora_cols = -(-lora_used // 256) * 256
    assert (3 * rw) % lora_cols == 0 and (dl + al) % LANES + fox_heads <= LANES
    gate_block = (dl + al) // LANES
    gate_off = (dl + al) % LANES

    xb = x.reshape(b * s, d).astype(BF16)
    w_bf = jnp.concatenate([w_in[:, :3 * fw], w_in[:, o_z:]], axis=1).astype(BF16)
    w_f32 = jnp.concatenate(
        [w_in[:, o_r:o_ad + al], w_in[:, o_ff:o_r], jnp.zeros((d, lora_cols - lora_used), w_in.dtype)],
        axis=1).astype(BF16)
    q_scale = jnp.concatenate([jnp.full((1, fw), FOX_HEAD_DIM ** -0.5, F32), jnp.ones((1, 2 * fw + mix), F32)], axis=1)

    qkvz = _matmul(xb, w_bf, BF16, 1024, 1024, col_scale=q_scale).reshape(b, s, 3 * fw + mix)
    rkvl = _matmul(xb, w_f32, F32, 512, 1280).reshape(b, s, 3 * rw + lora_cols)

    bias_row = jnp.zeros((1, LANES), F32).at[0, gate_off:gate_off + fox_heads].set(f_bias.astype(F32))
    c_full = _gate_cumsum(rkvl, bias_row, 3 * rw // LANES + gate_block, 512)
    tq = min(512, s)
    c5 = jnp.transpose(c_full[:, :, gate_off:gate_off + fox_heads], (0, 2, 1)).reshape(b, fox_heads, s // tq, 1, tq)
    hf = _fox_attention(qkvz, c5, fox_heads, tq)

    pad_rows = lambda w, off: jnp.zeros((lora_cols, rw), F32).at[off:off + w.shape[0]].set(w.astype(F32)).astype(BF16)
    mu_row = jnp.concatenate([mu_shift.astype(F32), jnp.zeros((lora_cols - dl - al,), F32)]).reshape(1, -1)
    rows = {"mu": mu_row}
    for name, val in (("w0", w0), ("a0", a0), ("k_k", k_k), ("k_a", k_a), ("r_k", r_k),
                      ("gn_gain", gn_gain), ("gn_bias", gn_bias)):
        rows[name] = val.astype(F32).reshape(1, rw)
    hr = _rwkv_mix(rkvl, qkvz, rows, pad_rows(w_up, 0), pad_rows(a_up, dl), n_pairs, lora_cols,
                   (3 * fw + fw) // LANES, 256)

    out = _out_proj_layernorm(hf.reshape(b * s, fw), hr.reshape(b * s, rw), w_out.astype(BF16),
                              x.reshape(b * s, d), ln_gain, ln_bias, alpha, 512, 512)
    return out.reshape(b, s, d)
```

```python
import functools

import jax
import jax.numpy as jnp
from jax import lax
from jax.experimental import pallas as pl
from jax.experimental.pallas import tpu as pltpu

F32 = jnp.float32
BF16 = jnp.bfloat16

LANES = 128
BF16_SUBLANES = 16
FOX_HEAD_DIM = 128
RWKV_HEAD_DIM = 64
RWKV_CHUNK = 64
LOG2_E = 1.4426950408889634
LN_EPS = 1e-5
GN_EPS = 64e-5
VMEM_LIMIT_BYTES = 56 * 1024 * 1024


def _bdot(a, b, dims=(((1,), (0,)), ((), ()))):
    return lax.dot_general(a.astype(BF16), b.astype(BF16), dims, preferred_element_type=F32)


def _split3_dot(m_bf16, x):
    hi = x.astype(BF16)
    r1 = x - hi.astype(F32)
    mid = r1.astype(BF16)
    lo = (r1 - mid.astype(F32)).astype(BF16)
    dot = functools.partial(jnp.dot, preferred_element_type=F32)
    return dot(m_bf16, hi) + dot(m_bf16, mid) + dot(m_bf16, lo)


def _split2_dot_right(x, m_bf16):
    hi = x.astype(BF16)
    lo = (x - hi.astype(F32)).astype(BF16)
    dot = functools.partial(jnp.dot, preferred_element_type=F32)
    return dot(hi, m_bf16) + dot(lo, m_bf16)


def _pick_tile(n, preferred):
    t = min(preferred, n) // LANES * LANES
    while n % t:
        t -= LANES
    return t


def _mm_kernel(a_ref, b_ref, o_ref):
    o_ref[...] = jnp.dot(a_ref[...], b_ref[...], preferred_element_type=F32).astype(o_ref.dtype)


def _mm_scaled_kernel(a_ref, b_ref, s_ref, o_ref):
    acc = jnp.dot(a_ref[...], b_ref[...], preferred_element_type=F32)
    o_ref[...] = (acc * s_ref[...]).astype(o_ref.dtype)


def _matmul(a, b, out_dtype, tm, tn, col_scale=None):
    m, k = a.shape
    n = b.shape[1]
    tm, tn = _pick_tile(m, tm), _pick_tile(n, tn)
    in_specs = [pl.BlockSpec((tm, k), lambda i, j: (i, 0)),
                pl.BlockSpec((k, tn), lambda i, j: (0, j))]
    args = [a, b]
    body = _mm_kernel
    if col_scale is not None:
        in_specs.append(pl.BlockSpec((1, tn), lambda i, j: (0, j)))
        args.append(col_scale)
        body = _mm_scaled_kernel
    return pl.pallas_call(
        body,
        grid=(m // tm, n // tn),
        in_specs=in_specs,
        out_specs=pl.BlockSpec((tm, tn), lambda i, j: (i, j)),
        out_shape=jax.ShapeDtypeStruct((m, n), out_dtype),
        name="in_proj_scaled" if col_scale is not None else "in_proj",
        compiler_params=pltpu.CompilerParams(
            dimension_semantics=("parallel", "arbitrary"), vmem_limit_bytes=VMEM_LIMIT_BYTES),
    )(*args)


def _gate_kernel(x_ref, b_ref, o_ref, carry_ref):
    @pl.when(pl.program_id(1) == 0)
    def _():
        carry_ref[...] = jnp.zeros_like(carry_ref)

    t = x_ref.shape[1]
    x = x_ref[0] + b_ref[...]
    logf = jnp.minimum(x, 0.0) - jnp.log1p(jnp.exp(-jnp.abs(x)))
    tri = (lax.broadcasted_iota(jnp.int32, (t, t), 1)
           <= lax.broadcasted_iota(jnp.int32, (t, t), 0)).astype(BF16)
    c = _split3_dot(tri, logf) + carry_ref[0:1, :]
    o_ref[0] = c * LOG2_E
    carry_ref[0:1, :] = c[t - 1:t, :]


def _gate_cumsum(lora3, bias_row, col_block, t):
    b, s, _ = lora3.shape
    t = min(t, s)
    return pl.pallas_call(
        _gate_kernel,
        grid=(b, s // t),
        in_specs=[pl.BlockSpec((1, t, LANES), lambda bi, i: (bi, i, col_block)),
                  pl.BlockSpec((1, LANES), lambda bi, i: (0, 0))],
        out_specs=pl.BlockSpec((1, t, LANES), lambda bi, i: (bi, i, 0)),
        out_shape=jax.ShapeDtypeStruct((b, s, LANES), F32),
        name="fox_gate_cumsum",
        scratch_shapes=[pltpu.VMEM((8, LANES), F32)],
        compiler_params=pltpu.CompilerParams(dimension_semantics=("parallel", "arbitrary")),
    )(lora3, bias_row)


def _fox_kernel(q_ref, k_ref, v_ref, c_ref, z_ref, o_ref, kaug_ref, vt_ref, sa_ref, sb_ref, pa_ref, pb_ref,
                m_ref, al_ref, acc_ref, *, tq, c_lane0, group_cols):
    head = pl.program_id(1)
    i = pl.program_id(2)
    s_len = k_ref.shape[1]
    dh = FOX_HEAD_DIM

    @pl.when(i == 0)
    def _():
        lane_r = lax.broadcasted_iota(jnp.int32, (LANES, LANES), 0)
        lane_c = lax.broadcasted_iota(jnp.int32, (LANES, LANES), 1)
        pick = [jnp.where((lane_r == c_lane0 + head) & (lane_c == term), 1.0, 0.0).astype(BF16) for term in range(3)]
        for jb in range(s_len // tq):
            sl = slice(jb * tq, (jb + 1) * tq)
            c = c_ref[0, sl, :]
            hi = c.astype(BF16)
            r1 = c - hi.astype(F32)
            mid = r1.astype(BF16)
            lo = (r1 - mid.astype(F32)).astype(BF16)
            dot = functools.partial(jnp.dot, preferred_element_type=F32)
            c_terms = dot(hi, pick[0]) + dot(mid, pick[1]) + dot(lo, pick[2])
            kaug_ref[sl, :dh] = k_ref[0, sl, :]
            kaug_ref[sl, dh:] = (-c_terms).astype(BF16)
            vt_ref[jb, :dh, :] = v_ref[0, sl, :].astype(F32).T.astype(BF16)
            vt_ref[jb, dh:, :] = jnp.ones((vt_ref.shape[1] - dh, tq), BF16)

    ones3 = jnp.where(lax.broadcasted_iota(jnp.int32, (dh, tq), 0) < 3, 1.0, 0.0)
    q_t = jnp.concatenate([q_ref[0].astype(F32).T, ones3], axis=0).astype(BF16)
    gw = group_cols
    groups = range(tq // gw)
    q_g = [q_t[:, g * gw:(g + 1) * gw] for g in groups]

    def scores(j, s_out):
        kb = kaug_ref[pl.ds(pl.multiple_of(j * tq, tq), tq), :]
        for g in groups:
            s_out[:, g * gw:(g + 1) * gw] = jnp.dot(kb, q_g[g], preferred_element_type=F32)

    def accumulate(p, j):
        vb = vt_ref[j]
        for g in groups:
            gs = slice(g * gw, (g + 1) * gw)
            acc_ref[:, gs] = al_ref[:, gs] * acc_ref[:, gs] + jnp.dot(vb, p[g], preferred_element_type=F32)

    def softmax_step(s):
        p = []
        for g in groups:
            gs = slice(g * gw, (g + 1) * gw)
            m = m_ref[:, gs]
            m_new = jnp.maximum(m, jnp.max(s[g], axis=0, keepdims=True))
            p.append(jnp.exp2(s[g] - m_new).astype(BF16))
            al_ref[:, gs] = jnp.exp2(m - m_new)
            m_ref[:, gs] = m_new
        return p

    def step(j, s_in, s_out, p_in, p_out):
        scores(j + 1, s_out)
        accumulate([p_in[:, g * gw:(g + 1) * gw] for g in groups], jnp.maximum(j - 1, 0))
        p = softmax_step([s_in[:, g * gw:(g + 1) * gw] for g in groups])
        for g in groups:
            p_out[:, g * gw:(g + 1) * gw] = p[g]

    scores(0, sa_ref)
    pa_ref[...] = jnp.zeros_like(pa_ref)
    m_ref[...] = jnp.full(m_ref.shape, -jnp.inf, F32)
    al_ref[...] = jnp.ones_like(al_ref)
    acc_ref[...] = jnp.zeros_like(acc_ref)

    def pair(t, carry):
        step(2 * t, sa_ref, sb_ref, pa_ref, pb_ref)
        step(2 * t + 1, sb_ref, sa_ref, pb_ref, pa_ref)
        return carry

    lax.fori_loop(0, i // 2, pair, 0)

    @pl.when(i % 2 == 1)
    def _():
        step(i - 1, sa_ref, sb_ref, pa_ref, pb_ref)
        sa_ref[...] = sb_ref[...]
        pa_ref[...] = pb_ref[...]

    accumulate([pa_ref[:, g * gw:(g + 1) * gw] for g in groups], jnp.maximum(i - 1, 0))
    key = lax.broadcasted_iota(jnp.int32, (tq, gw), 0)
    qry = lax.broadcasted_iota(jnp.int32, (tq, gw), 1)
    p = softmax_step([jnp.where(key <= qry + g * gw, sa_ref[:, g * gw:(g + 1) * gw], -jnp.inf)
                      for g in groups])
    accumulate(p, i)
    for g in groups:
        acc_g = acc_ref[:, g * gw:(g + 1) * gw]
        z = z_ref[0, g * gw:(g + 1) * gw, :].astype(F32)
        gate = z / (1.0 + jnp.exp(-z))
        out = (acc_g[:dh, :] / acc_g[dh:dh + 1, :]).T
        o_ref[0, g * gw:(g + 1) * gw, :] = (out * gate).astype(o_ref.dtype)


def _fox_attention(qkvz, c2, n_heads, tq, c_lane0):
    b, s, _ = qkvz.shape
    nq = s // tq
    h = n_heads
    return pl.pallas_call(
        functools.partial(_fox_kernel, tq=tq, c_lane0=c_lane0, group_cols=min(256, tq)),
        grid=(b, h, nq),
        in_specs=[pl.BlockSpec((1, tq, FOX_HEAD_DIM), lambda bi, hi, i: (bi, i, hi)),
                  pl.BlockSpec((1, s, FOX_HEAD_DIM), lambda bi, hi, i: (bi, 0, h + hi)),
                  pl.BlockSpec((1, s, FOX_HEAD_DIM), lambda bi, hi, i: (bi, 0, 2 * h + hi)),
                  pl.BlockSpec((1, s, LANES), lambda bi, hi, i: (bi, 0, 0)),
                  pl.BlockSpec((1, tq, FOX_HEAD_DIM), lambda bi, hi, i: (bi, i, 3 * h + hi))],
        out_specs=pl.BlockSpec((1, tq, FOX_HEAD_DIM), lambda bi, hi, i: (bi, i, hi)),
        out_shape=jax.ShapeDtypeStruct((b, s, h * FOX_HEAD_DIM), BF16),
        name="fox_attention",
        scratch_shapes=[pltpu.VMEM((s, 2 * FOX_HEAD_DIM), BF16),
                        pltpu.VMEM((nq, FOX_HEAD_DIM + BF16_SUBLANES, tq), BF16),
                        pltpu.VMEM((tq, tq), F32), pltpu.VMEM((tq, tq), F32),
                        pltpu.VMEM((tq, tq), BF16), pltpu.VMEM((tq, tq), BF16),
                        pltpu.VMEM((1, tq), F32), pltpu.VMEM((1, tq), F32),
                        pltpu.VMEM((FOX_HEAD_DIM + BF16_SUBLANES, tq), F32)],
        compiler_params=pltpu.CompilerParams(
            dimension_semantics=("parallel", "parallel", "arbitrary"), vmem_limit_bytes=VMEM_LIMIT_BYTES),
    )(qkvz, qkvz, qkvz, c2, qkvz)


def _stack_heads(x, head0):
    zero = jnp.zeros_like(x)
    return jnp.concatenate([jnp.where(head0, x, zero), jnp.where(head0, zero, x)], axis=0)


def _rwkv_chunks(units, masks):
    head0, strict, incl, blockdiag, eye = masks
    c = units[0]["r"].shape[0]
    n_units = range(len(units))
    zero = jnp.zeros((c, LANES), F32)
    nt = (((1,), (1,)), ((), ()))
    tn = (((0,), (0,)), ((), ()))

    rt, at, bh, kh, gc, sv, lhs1, rhs1 = [], [], [], [], [], [], [], []
    for u in units:
        lg = u["lg"]
        lgc = lg[c - 1:c, :]
        e_neg = jnp.exp(-lg)
        e_end = jnp.exp(lgc - lg)
        rt.append(u["r"] * jnp.exp(lg))
        at.append(u["av"] * jnp.exp(lg - u["logw"]))
        bh.append(u["bv"] * e_end)
        kh.append(u["k2"] * e_end)
        gc.append(jnp.exp(lgc))
        sv.append(_stack_heads(u["v"], head0))
        lhs1.append(jnp.concatenate([at[-1], rt[-1]], axis=0))
        rhs1.append(jnp.concatenate([_stack_heads(u["bv"] * e_neg, head0),
                                     _stack_heads(u["k2"] * e_neg, head0)], axis=0))

    big = [_bdot(lhs1[i], rhs1[i], nt) for i in n_units]
    l_mat = [jnp.where(strict, big[i][:c, :LANES], zero) for i in n_units]
    a_ak = [jnp.where(strict, big[i][:c, LANES:], zero) for i in n_units]
    a_r = [jnp.concatenate([jnp.where(incl, big[i][c:, :LANES], zero),
                            jnp.where(incl, big[i][c:, LANES:], zero)], axis=1) for i in n_units]

    aakv = [_bdot(a_ak[i], sv[i]) for i in n_units]
    x = [jnp.concatenate([at[i], aakv[i]], axis=1) for i in n_units]
    lp = l_mat
    n_steps = max(1, (c - 1).bit_length())
    for step in range(n_steps):
        sx = [jnp.concatenate([_stack_heads(x[i][:, :LANES], head0), _stack_heads(x[i][:, LANES:], head0)], axis=1)
              for i in n_units]
        x = [x[i] + _bdot(lp[i], sx[i]) for i in n_units]
        if step + 1 < n_steps:
            lp = [_bdot(lp[i], _stack_heads(lp[i], head0)) for i in n_units]

    rhs2 = [jnp.concatenate(
        [jnp.concatenate([_stack_heads(x[i][:, :LANES], head0), _stack_heads(x[i][:, LANES:], head0)], axis=1),
         jnp.concatenate([jnp.zeros((2 * c, LANES), F32), sv[i]], axis=1)], axis=0) for i in n_units]
    qy = [_bdot(a_r[i], rhs2[i]) for i in n_units]
    gh = [_bdot(jnp.concatenate([bh[i], kh[i]], axis=0),
                jnp.concatenate([x[i], jnp.concatenate([zero, units[i]["v"]], axis=1)], axis=0), tn)
          for i in n_units]
    out = []
    for i in n_units:
        out.append((rt[i] + qy[i][:, :LANES], qy[i][:, LANES:],
                    jnp.where(blockdiag, gh[i][:, :LANES], 0.0) + eye * gc[i],
                    jnp.where(blockdiag, gh[i][:, LANES:], 0.0)))
    return out


def _rwkv_kernel(r_ref, k_ref, v_ref, lo_ref, z_ref, mur_ref, muk_ref, muv_ref, mulo_ref,
                 wup_ref, aup_ref, w0_ref, a0_ref, kk_ref, ka_ref, rk_ref, gg_ref, gb_ref,
                 o_ref, prev_ref, sbd_ref):
    nb, t, w = r_ref.shape
    c = RWKV_CHUNK
    n = RWKV_HEAD_DIM
    n_pairs = w // LANES
    n_chunks = t // c

    @pl.when(pl.program_id(1) == 0)
    def _():
        prev_ref[...] = jnp.zeros_like(prev_ref)
        sbd_ref[...] = jnp.zeros_like(sbd_ref)

    first_row = lax.broadcasted_iota(jnp.int32, (t, 1), 0) == 0
    wr = lax.broadcasted_iota(jnp.int32, (w, w), 0)
    wc = lax.broadcasted_iota(jnp.int32, (w, w), 1)
    head_ones = ((wr // n) == (wc // n)).astype(BF16)
    tr = lax.broadcasted_iota(jnp.int32, (t, t), 0)
    tc = lax.broadcasted_iota(jnp.int32, (t, t), 1)
    chunk_tri = ((tc <= tr) & ((tc // c) == (tr // c))).astype(BF16)

    def head_sum(x):
        return _split2_dot_right(x, head_ones)

    def token_shift(x, mu, bi, lo, hi):
        prev = jnp.where(first_row, prev_ref[bi:bi + 1, lo:hi], pltpu.roll(x, 1, axis=0))
        prev_ref[bi:bi + 1, lo:hi] = x[t - 1:t, :]
        return x + (prev - x) * mu

    prepped = []
    for bi in range(nb):
        r = token_shift(r_ref[bi], mur_ref[...], bi, 0, w)
        k = token_shift(k_ref[bi], muk_ref[...], bi, w, 2 * w)
        v = token_shift(v_ref[bi], muv_ref[...], bi, 2 * w, 3 * w)
        lora = token_shift(lo_ref[bi], mulo_ref[...], bi, 3 * w, 3 * w + lo_ref.shape[2])
        dec = _bdot(jnp.tanh(lora), wup_ref[...])
        wx = -(w0_ref[...] + dec)
        w_raw = -(jnp.maximum(wx, 0.0) + jnp.log1p(jnp.exp(-jnp.abs(wx)))) - 0.5
        logw = -jnp.exp(w_raw)
        a = 1.0 / (1.0 + jnp.exp(-(a0_ref[...] + _bdot(lora, aup_ref[...]))))
        kk = k * kk_ref[...]
        kk = kk / jnp.maximum(jnp.sqrt(head_sum(kk * kk)), 1e-12)
        k2 = k * (1.0 + (a - 1.0) * ka_ref[...])
        prepped.append(dict(r=r, k2=k2, v=v, av=-kk, bv=kk * a, logw=logw, lg=_split3_dot(chunk_tri, logw)))

    lane = lax.broadcasted_iota(jnp.int32, (c, LANES), 1)
    row = lax.broadcasted_iota(jnp.int32, (c, LANES), 0)
    head0 = lane < n
    col = jnp.where(head0, lane, lane - n)
    lane_r = lax.broadcasted_iota(jnp.int32, (LANES, LANES), 0)
    lane_c = lax.broadcasted_iota(jnp.int32, (LANES, LANES), 1)
    masks = (head0, col < row, col <= row, (lane_r < n) == (lane_c < n), (lane_r == lane_c).astype(F32))

    seqs = [(bi, pi) for bi in range(nb) for pi in range(n_pairs)]
    units = [{name: arr[ci * c:(ci + 1) * c, pi * LANES:(pi + 1) * LANES] for name, arr in prepped[bi].items()}
             for ci in range(n_chunks) for (bi, pi) in seqs]
    affine = _rwkv_chunks(units, masks)

    sbd = [sbd_ref[si] for si in range(len(seqs))]
    ys = {}
    for ci in range(n_chunks):
        qs = [affine[ci * len(seqs) + si] for si in range(len(seqs))]
        y_c = [_bdot(qs[si][0], sbd[si]) + qs[si][1] for si in range(len(seqs))]
        sbd = [_bdot(qs[si][2], sbd[si]) + qs[si][3] for si in range(len(seqs))]
        for si, seq in enumerate(seqs):
            ys[seq + (ci,)] = y_c[si]
    for si in range(len(seqs)):
        sbd_ref[si] = sbd[si]

    for bi in range(nb):
        y = jnp.concatenate([jnp.concatenate([ys[(bi, pi, ci)] for pi in range(n_pairs)], axis=1)
                             for ci in range(n_chunks)], axis=0)
        p = prepped[bi]
        mu = head_sum(y) * (1.0 / n)
        d = y - mu
        var = head_sum(d * d) * (1.0 / n)
        y = d * lax.rsqrt(var + GN_EPS) * gg_ref[...] + gb_ref[...]
        y = y + head_sum(p["r"] * p["k2"] * rk_ref[...]) * p["v"]
        z = z_ref[bi].astype(F32)
        o_ref[bi] = (y * (z / (1.0 + jnp.exp(-z)))).astype(o_ref.dtype)


def _rwkv_mix(rkvl, qkvz, rows, wup_pad, aup_pad, n_pairs, lora_cols, z_col0, t, pairs_per_step):
    b, s, _ = rkvl.shape
    t = min(t, s)
    w = pairs_per_step * LANES
    p = n_pairs // pairs_per_step
    assert p * pairs_per_step == n_pairs and z_col0 % w == 0
    lora_block = 3 * n_pairs * LANES // lora_cols
    assert lora_block * lora_cols == 3 * n_pairs * LANES
    tok = lambda off: pl.BlockSpec((b, t, w), lambda j, i: (0, i, off + j))
    prow = lambda off: pl.BlockSpec((1, w), lambda j, i: (0, off + j))
    in_specs = [
        tok(0), tok(p), tok(2 * p),
        pl.BlockSpec((b, t, lora_cols), lambda j, i: (0, i, lora_block)),
        tok(z_col0 // w),
        prow(0), prow(p), prow(2 * p),
        pl.BlockSpec((1, lora_cols), lambda j, i: (0, lora_block)),
        pl.BlockSpec((lora_cols, w), lambda j, i: (0, j)),
        pl.BlockSpec((lora_cols, w), lambda j, i: (0, j)),
    ] + [prow(0)] * 7
    return pl.pallas_call(
        _rwkv_kernel,
        grid=(p, s // t),
        in_specs=in_specs,
        out_specs=pl.BlockSpec((b, t, w), lambda j, i: (0, i, j)),
        out_shape=jax.ShapeDtypeStruct((b, s, n_pairs * LANES), BF16),
        name="rwkv7_mix",
        scratch_shapes=[pltpu.VMEM((8, 3 * w + lora_cols), F32),
                        pltpu.VMEM((b * pairs_per_step, LANES, LANES), F32)],
        compiler_params=pltpu.CompilerParams(
            dimension_semantics=("parallel", "arbitrary"), vmem_limit_bytes=VMEM_LIMIT_BYTES),
    )(rkvl, rkvl, rkvl, rkvl, qkvz, rows["mu"], rows["mu"], rows["mu"], rows["mu"], wup_pad, aup_pad,
      rows["w0"], rows["a0"], rows["k_k"], rows["k_a"], rows["r_k"], rows["gn_gain"], rows["gn_bias"])


def _out_kernel(hf_ref, hr_ref, wt_ref, wb_ref, x_ref, g_ref, b_ref, o_ref, acc_ref, *, alpha, n_tiles):
    j = pl.program_id(1)
    out = (jnp.dot(hf_ref[...], wt_ref[...], preferred_element_type=F32)
           + jnp.dot(hr_ref[...], wb_ref[...], preferred_element_type=F32))
    acc_ref[j] = alpha * x_ref[...] + out

    @pl.when(j == n_tiles - 1)
    def _():
        tn = acc_ref.shape[2]
        d = n_tiles * tn
        total = acc_ref[0]
        for jj in range(1, n_tiles):
            total = total + acc_ref[jj]
        mean = jnp.sum(total, axis=-1, keepdims=True) * (1.0 / d)
        sq = jnp.square(acc_ref[0] - mean)
        for jj in range(1, n_tiles):
            sq = sq + jnp.square(acc_ref[jj] - mean)
        inv = lax.rsqrt(jnp.sum(sq, axis=-1, keepdims=True) * (1.0 / d) + LN_EPS)
        for jj in range(n_tiles):
            sl = slice(jj * tn, (jj + 1) * tn)
            o_ref[:, sl] = ((acc_ref[jj] - mean) * inv * g_ref[:, sl] + b_ref[:, sl]).astype(o_ref.dtype)


def _out_proj_layernorm(hf, hr, w_out_bf16, x2, ln_gain, ln_bias, alpha, tm, tn):
    m, d = x2.shape
    kf, kr = hf.shape[1], hr.shape[1]
    assert kf == kr and kf + kr == w_out_bf16.shape[0]
    tm, tn = min(tm, m), min(tn, d)
    n_tiles = d // tn
    return pl.pallas_call(
        functools.partial(_out_kernel, alpha=alpha, n_tiles=n_tiles),
        grid=(m // tm, n_tiles),
        in_specs=[pl.BlockSpec((tm, kf), lambda i, j: (i, 0)),
                  pl.BlockSpec((tm, kr), lambda i, j: (i, 0)),
                  pl.BlockSpec((kf, tn), lambda i, j: (0, j)),
                  pl.BlockSpec((kr, tn), lambda i, j: (1, j)),
                  pl.BlockSpec((tm, tn), lambda i, j: (i, j)),
                  pl.BlockSpec((1, d), lambda i, j: (0, 0)),
                  pl.BlockSpec((1, d), lambda i, j: (0, 0))],
        out_specs=pl.BlockSpec((tm, d), lambda i, j: (i, 0)),
        out_shape=jax.ShapeDtypeStruct((m, d), x2.dtype),
        name="out_proj_layernorm",
        scratch_shapes=[pltpu.VMEM((n_tiles, tm, tn), F32)],
        compiler_params=pltpu.CompilerParams(
            dimension_semantics=("parallel", "arbitrary"), vmem_limit_bytes=VMEM_LIMIT_BYTES),
    )(hf, hr, w_out_bf16, w_out_bf16, x2, ln_gain.reshape(1, d).astype(F32), ln_bias.reshape(1, d).astype(F32))


def kernel(x, w_in, f_bias, mu_shift, w0, w_up, a0, a_up, k_k, k_a, r_k, gn_gain, gn_bias, w_out, ln_gain, ln_bias):
    b, s, d = x.shape
    fox_heads = f_bias.shape[0]
    fw = fox_heads * FOX_HEAD_DIM
    rw = w0.shape[0]
    n_pairs = rw // LANES
    assert r_k.shape[1] == RWKV_HEAD_DIM and rw % LANES == 0
    dl, al = w_up.shape[0], a_up.shape[0]
    mix = fw + rw
    assert w_in.shape[1] == 3 * fw + fox_heads + 3 * rw + dl + al + mix
    depth = 1
    alpha = (2 * depth) ** 0.25

    o_ff = 3 * fw
    o_r = o_ff + fox_heads
    o_wd = o_r + 3 * rw
    o_ad = o_wd + dl
    o_z = o_ad + al

    lora_used = dl + al + fox_heads
    lora_cols = -(-lora_used // 256) * 256
    assert (3 * rw) % lora_cols == 0 and (dl + al) % LANES + fox_heads <= LANES
    gate_block = (dl + al) // LANES
    gate_off = (dl + al) % LANES

    xb = x.reshape(b * s, d).astype(BF16)
    w_bf = jnp.concatenate([w_in[:, :3 * fw], w_in[:, o_z:]], axis=1).astype(BF16)
    w_f32 = jnp.concatenate(
        [w_in[:, o_r:o_ad + al], w_in[:, o_ff:o_r], jnp.zeros((d, lora_cols - lora_used), w_in.dtype)],
        axis=1).astype(BF16)
    q_scale = jnp.concatenate([jnp.full((1, fw), LOG2_E * FOX_HEAD_DIM ** -0.5, F32),
                               jnp.ones((1, 2 * fw + mix), F32)], axis=1)

    qkvz = _matmul(xb, w_bf, BF16, 1024, 1024, col_scale=q_scale).reshape(b, s, 3 * fw + mix)
    rkvl = _matmul(xb, w_f32, F32, 512, 1280).reshape(b, s, 3 * rw + lora_cols)

    bias_row = jnp.zeros((1, LANES), F32).at[0, gate_off:gate_off + fox_heads].set(f_bias.astype(F32))
    c2 = _gate_cumsum(rkvl, bias_row, 3 * rw // LANES + gate_block, 512)
    hf = _fox_attention(qkvz, c2, fox_heads, min(512, s), gate_off)

    pad_rows = lambda w, off: jnp.zeros((lora_cols, rw), F32).at[off:off + w.shape[0]].set(w.astype(F32)).astype(BF16)
    mu_row = jnp.concatenate([mu_shift.astype(F32), jnp.zeros((lora_cols - dl - al,), F32)]).reshape(1, -1)
    rows = {"mu": mu_row}
    for name, val in (("w0", w0), ("a0", a0), ("k_k", k_k), ("k_a", k_a), ("r_k", r_k),
                      ("gn_gain", gn_gain), ("gn_bias", gn_bias)):
        rows[name] = val.astype(F32).reshape(1, rw)
    hr = _rwkv_mix(rkvl, qkvz, rows, pad_rows(w_up, 0), pad_rows(a_up, dl), n_pairs, lora_cols,
                   3 * fw + fw, 128, 2)

    out = _out_proj_layernorm(hf.reshape(b * s, fw), hr.reshape(b * s, rw), w_out.astype(BF16),
                              x.reshape(b * s, d), ln_gain, ln_bias, alpha, 512, 512)
    return out.reshape(b, s, d)
```

```python
import functools

import jax
import jax.numpy as jnp
from jax import lax
from jax.experimental import pallas as pl
from jax.experimental.pallas import tpu as pltpu

F32 = jnp.float32
BF16 = jnp.bfloat16

LANES = 128
BF16_SUBLANES = 16
FOX_HEAD_DIM = 128
RWKV_HEAD_DIM = 64
RWKV_CHUNK = 64
LOG2_E = 1.4426950408889634
LN_EPS = 1e-5
GN_EPS = 64e-5
VMEM_LIMIT_BYTES = 56 * 1024 * 1024


def _bdot(a, b, dims=(((1,), (0,)), ((), ()))):
    return lax.dot_general(a.astype(BF16), b.astype(BF16), dims, preferred_element_type=F32)


def _split3_dot(m_bf16, x):
    hi = x.astype(BF16)
    r1 = x - hi.astype(F32)
    mid = r1.astype(BF16)
    lo = (r1 - mid.astype(F32)).astype(BF16)
    dot = functools.partial(jnp.dot, preferred_element_type=F32)
    return dot(m_bf16, hi) + dot(m_bf16, mid) + dot(m_bf16, lo)


def _split2_dot(m_bf16, x):
    hi = x.astype(BF16)
    lo = (x - hi.astype(F32)).astype(BF16)
    dot = functools.partial(jnp.dot, preferred_element_type=F32)
    return dot(m_bf16, hi) + dot(m_bf16, lo)


def _pick_tile(n, preferred):
    t = min(preferred, n) // LANES * LANES
    while n % t:
        t -= LANES
    return t


def _mm_kernel(a_ref, b_ref, o_ref):
    o_ref[...] = jnp.dot(a_ref[...], b_ref[...], preferred_element_type=F32).astype(o_ref.dtype)


def _mm_scaled_kernel(a_ref, b_ref, s_ref, o_ref):
    acc = jnp.dot(a_ref[...], b_ref[...], preferred_element_type=F32)
    o_ref[...] = (acc * s_ref[...]).astype(o_ref.dtype)


def _matmul(a, b, out_dtype, tm, tn, col_scale=None):
    m, k = a.shape
    n = b.shape[1]
    tm, tn = _pick_tile(m, tm), _pick_tile(n, tn)
    in_specs = [pl.BlockSpec((tm, k), lambda i, j: (i, 0)),
                pl.BlockSpec((k, tn), lambda i, j: (0, j))]
    args = [a, b]
    body = _mm_kernel
    if col_scale is not None:
        in_specs.append(pl.BlockSpec((1, tn), lambda i, j: (0, j)))
        args.append(col_scale)
        body = _mm_scaled_kernel
    return pl.pallas_call(
        body,
        grid=(m // tm, n // tn),
        in_specs=in_specs,
        out_specs=pl.BlockSpec((tm, tn), lambda i, j: (i, j)),
        out_shape=jax.ShapeDtypeStruct((m, n), out_dtype),
        name="in_proj_scaled" if col_scale is not None else "in_proj",
        compiler_params=pltpu.CompilerParams(
            dimension_semantics=("parallel", "arbitrary"), vmem_limit_bytes=VMEM_LIMIT_BYTES),
    )(*args)


def _gate_kernel(x_ref, b_ref, o_ref, carry_ref):
    @pl.when(pl.program_id(1) == 0)
    def _():
        carry_ref[...] = jnp.zeros_like(carry_ref)

    t = x_ref.shape[1]
    x = x_ref[0] + b_ref[...]
    logf = jnp.minimum(x, 0.0) - jnp.log1p(jnp.exp(-jnp.abs(x)))
    tri = (lax.broadcasted_iota(jnp.int32, (t, t), 1)
           <= lax.broadcasted_iota(jnp.int32, (t, t), 0)).astype(BF16)
    c = _split3_dot(tri, logf) + carry_ref[0:1, :]
    o_ref[0] = c * LOG2_E
    carry_ref[0:1, :] = c[t - 1:t, :]


def _gate_cumsum(lora3, bias_row, col_block, t):
    b, s, _ = lora3.shape
    t = min(t, s)
    return pl.pallas_call(
        _gate_kernel,
        grid=(b, s // t),
        in_specs=[pl.BlockSpec((1, t, LANES), lambda bi, i: (bi, i, col_block)),
                  pl.BlockSpec((1, LANES), lambda bi, i: (0, 0))],
        out_specs=pl.BlockSpec((1, t, LANES), lambda bi, i: (bi, i, 0)),
        out_shape=jax.ShapeDtypeStruct((b, s, LANES), F32),
        name="fox_gate_cumsum",
        scratch_shapes=[pltpu.VMEM((8, LANES), F32)],
        compiler_params=pltpu.CompilerParams(dimension_semantics=("parallel", "arbitrary")),
    )(lora3, bias_row)


def _fox_kernel(q_ref, k_ref, v_ref, c_ref, z_ref, o_ref, kaug_ref, vt_ref, sa_ref, sb_ref, pa_ref, pb_ref,
                m_ref, al_ref, acc_ref, *, tq, c_lane0, group_cols):
    i = pl.program_id(2)
    s_len = k_ref.shape[1]
    dh = FOX_HEAD_DIM
    n_heads = k_ref.shape[2] // dh
    gw = group_cols
    n_groups = tq // gw
    units = [(hh, g) for hh in range(n_heads) for g in range(n_groups)]
    cols = [slice(u * gw, (u + 1) * gw) for u in range(len(units))]

    @pl.when(i == 0)
    def _():
        lane_r = lax.broadcasted_iota(jnp.int32, (LANES, LANES), 0)
        lane_c = lax.broadcasted_iota(jnp.int32, (LANES, LANES), 1)
        dot = functools.partial(jnp.dot, preferred_element_type=F32)
        for hh in range(n_heads):
            c_lane = c_lane0 + pl.program_id(1) * n_heads + hh
            pick = [jnp.where((lane_r == c_lane) & (lane_c == term), 1.0, 0.0).astype(BF16) for term in range(3)]
            hs = slice(hh * dh, (hh + 1) * dh)
            for jb in range(s_len // tq):
                sl = slice(jb * tq, (jb + 1) * tq)
                c = c_ref[0, sl, :]
                hi = c.astype(BF16)
                r1 = c - hi.astype(F32)
                mid = r1.astype(BF16)
                lo = (r1 - mid.astype(F32)).astype(BF16)
                c_terms = dot(hi, pick[0]) + dot(mid, pick[1]) + dot(lo, pick[2])
                kaug_ref[hh, sl, :dh] = k_ref[0, sl, hs]
                kaug_ref[hh, sl, dh:] = (-c_terms).astype(BF16)
                vt_ref[hh, jb, :dh, :] = v_ref[0, sl, hs].astype(F32).T.astype(BF16)
                vt_ref[hh, jb, dh:, :] = jnp.ones((vt_ref.shape[2] - dh, tq), BF16)

    ones3 = jnp.where(lax.broadcasted_iota(jnp.int32, (dh, tq), 0) < 3, 1.0, 0.0)
    q_t = [jnp.concatenate([q_ref[0, :, hh * dh:(hh + 1) * dh].astype(F32).T, ones3], axis=0).astype(BF16)
           for hh in range(n_heads)]
    q_u = [q_t[hh][:, g * gw:(g + 1) * gw] for (hh, g) in units]

    def scores(j, s_out):
        rows = pl.ds(pl.multiple_of(j * tq, tq), tq)
        for u, (hh, _) in enumerate(units):
            s_out[:, cols[u]] = jnp.dot(kaug_ref[hh, rows, :], q_u[u], preferred_element_type=F32)

    def accumulate(p, j):
        for u, (hh, _) in enumerate(units):
            acc_ref[:, cols[u]] = (al_ref[:, cols[u]] * acc_ref[:, cols[u]]
                                   + jnp.dot(vt_ref[hh, j], p[u], preferred_element_type=F32))

    def softmax_step(s):
        p = []
        for u in range(len(units)):
            m = m_ref[:, cols[u]]
            m_new = jnp.maximum(m, jnp.max(s[u], axis=0, keepdims=True))
            p.append(jnp.exp2(s[u] - m_new).astype(BF16))
            al_ref[:, cols[u]] = jnp.exp2(m - m_new)
            m_ref[:, cols[u]] = m_new
        return p

    def step(j, s_in, s_out, p_in, p_out):
        scores(j + 1, s_out)
        accumulate([p_in[:, cs] for cs in cols], jnp.maximum(j - 1, 0))
        p = softmax_step([s_in[:, cs] for cs in cols])
        for u, cs in enumerate(cols):
            p_out[:, cs] = p[u]

    scores(0, sa_ref)
    pa_ref[...] = jnp.zeros_like(pa_ref)
    m_ref[...] = jnp.full(m_ref.shape, -jnp.inf, F32)
    al_ref[...] = jnp.ones_like(al_ref)
    acc_ref[...] = jnp.zeros_like(acc_ref)

    def pair(t, carry):
        step(2 * t, sa_ref, sb_ref, pa_ref, pb_ref)
        step(2 * t + 1, sb_ref, sa_ref, pb_ref, pa_ref)
        return carry

    lax.fori_loop(0, i // 2, pair, 0)

    @pl.when(i % 2 == 1)
    def _():
        step(i - 1, sa_ref, sb_ref, pa_ref, pb_ref)
        sa_ref[...] = sb_ref[...]
        pa_ref[...] = pb_ref[...]

    accumulate([pa_ref[:, cs] for cs in cols], jnp.maximum(i - 1, 0))
    key = lax.broadcasted_iota(jnp.int32, (tq, gw), 0)
    qry = lax.broadcasted_iota(jnp.int32, (tq, gw), 1)
    p = softmax_step([jnp.where(key <= qry + g * gw, sa_ref[:, cols[u]], -jnp.inf)
                      for u, (_, g) in enumerate(units)])
    accumulate(p, i)
    for u, (hh, g) in enumerate(units):
        acc_u = acc_ref[:, cols[u]]
        z = z_ref[0, g * gw:(g + 1) * gw, hh * dh:(hh + 1) * dh].astype(F32)
        gate = z / (1.0 + jnp.exp(-z))
        out = (acc_u[:dh, :] / acc_u[dh:dh + 1, :]).T
        o_ref[0, g * gw:(g + 1) * gw, hh * dh:(hh + 1) * dh] = (out * gate).astype(o_ref.dtype)


def _fox_attention(qkvz, c2, n_heads, tq, c_lane0, heads_per_step):
    b, s, _ = qkvz.shape
    nq = s // tq
    nh = heads_per_step
    hb = n_heads // nh
    assert hb * nh == n_heads
    w = nh * FOX_HEAD_DIM
    acc_rows = FOX_HEAD_DIM + BF16_SUBLANES
    return pl.pallas_call(
        functools.partial(_fox_kernel, tq=tq, c_lane0=c_lane0, group_cols=min(256, tq)),
        grid=(b, hb, nq),
        in_specs=[pl.BlockSpec((1, tq, w), lambda bi, hi, i: (bi, i, hi)),
                  pl.BlockSpec((1, s, w), lambda bi, hi, i: (bi, 0, hb + hi)),
                  pl.BlockSpec((1, s, w), lambda bi, hi, i: (bi, 0, 2 * hb + hi)),
                  pl.BlockSpec((1, s, LANES), lambda bi, hi, i: (bi, 0, 0)),
                  pl.BlockSpec((1, tq, w), lambda bi, hi, i: (bi, i, 3 * hb + hi))],
        out_specs=pl.BlockSpec((1, tq, w), lambda bi, hi, i: (bi, i, hi)),
        out_shape=jax.ShapeDtypeStruct((b, s, n_heads * FOX_HEAD_DIM), BF16),
        name="fox_attention",
        scratch_shapes=[pltpu.VMEM((nh, s, 2 * FOX_HEAD_DIM), BF16),
                        pltpu.VMEM((nh, nq, acc_rows, tq), BF16),
                        pltpu.VMEM((tq, nh * tq), F32), pltpu.VMEM((tq, nh * tq), F32),
                        pltpu.VMEM((tq, nh * tq), BF16), pltpu.VMEM((tq, nh * tq), BF16),
                        pltpu.VMEM((1, nh * tq), F32), pltpu.VMEM((1, nh * tq), F32),
                        pltpu.VMEM((acc_rows, nh * tq), F32)],
        compiler_params=pltpu.CompilerParams(
            dimension_semantics=("parallel", "parallel", "arbitrary"), vmem_limit_bytes=VMEM_LIMIT_BYTES),
    )(qkvz, qkvz, qkvz, c2, qkvz)


def _stack_heads(x, head0):
    zero = jnp.zeros_like(x)
    return jnp.concatenate([jnp.where(head0, x, zero), jnp.where(head0, zero, x)], axis=0)


def _rwkv_chunks(units, masks):
    head0, strict, diag, blockdiag, eye = masks
    incl = strict | diag
    c = units[0]["r"].shape[0]
    n_units = range(len(units))
    zero = jnp.zeros((c, LANES), F32)
    nt = (((1,), (1,)), ((), ()))
    tn = (((0,), (0,)), ((), ()))

    rt, at, bh, kh, gc, sv, lhs1, rhs1 = [], [], [], [], [], [], [], []
    for u in units:
        lg = u["lg"]
        lgc = lg[c - 1:c, :]
        e_neg = jnp.exp(-lg)
        e_end = jnp.exp(lgc - lg)
        rt.append(u["r"] * jnp.exp(lg))
        at.append(u["av"] * jnp.exp(lg - u["logw"]))
        bh.append(u["bv"] * e_end)
        kh.append(u["k2"] * e_end)
        gc.append(jnp.exp(lgc))
        sv.append(_stack_heads(u["v"], head0))
        lhs1.append(jnp.concatenate([at[-1], rt[-1]], axis=0))
        rhs1.append(jnp.concatenate([_stack_heads(u["bv"] * e_neg, head0),
                                     _stack_heads(u["k2"] * e_neg, head0)], axis=0))

    big = [_bdot(lhs1[i], rhs1[i], nt) for i in n_units]
    l_mat = [jnp.where(strict, big[i][:c, :LANES], zero) for i in n_units]
    a_ak = [jnp.where(strict, big[i][:c, LANES:], zero) for i in n_units]
    a_r = [jnp.concatenate([jnp.where(incl, big[i][c:, :LANES], zero),
                            jnp.where(incl, big[i][c:, LANES:], zero)], axis=1) for i in n_units]

    aakv = [_bdot(a_ak[i], sv[i]) for i in n_units]
    n_steps = max(1, (c - 1).bit_length())
    t_inv = [jnp.where(diag, 1.0, 0.0) + l_mat[i] for i in n_units]
    cur = [_bdot(l_mat[i], _stack_heads(l_mat[i], head0)) for i in n_units] if n_steps > 1 else None
    for step in range(1, n_steps):
        if step + 1 < n_steps:
            both = [_bdot(jnp.concatenate([cur[i], t_inv[i]], axis=0), _stack_heads(cur[i], head0)) for i in n_units]
            t_inv = [t_inv[i] + both[i][c:] for i in n_units]
            cur = [both[i][:c] for i in n_units]
        else:
            t_inv = [t_inv[i] + _bdot(t_inv[i], _stack_heads(cur[i], head0)) for i in n_units]
    x = [_bdot(t_inv[i], jnp.concatenate([_stack_heads(at[i], head0), _stack_heads(aakv[i], head0)], axis=1))
         for i in n_units]

    rhs2 = [jnp.concatenate(
        [jnp.concatenate([_stack_heads(x[i][:, :LANES], head0), _stack_heads(x[i][:, LANES:], head0)], axis=1),
         jnp.concatenate([jnp.zeros((2 * c, LANES), F32), sv[i]], axis=1)], axis=0) for i in n_units]
    qy = [_bdot(a_r[i], rhs2[i]) for i in n_units]
    gh = [_bdot(jnp.concatenate([bh[i], kh[i]], axis=0),
                jnp.concatenate([x[i], jnp.concatenate([zero, units[i]["v"]], axis=1)], axis=0), tn)
          for i in n_units]
    out = []
    for i in n_units:
        out.append((rt[i] + qy[i][:, :LANES], qy[i][:, LANES:],
                    jnp.where(blockdiag, gh[i][:, :LANES], 0.0) + eye * gc[i],
                    jnp.where(blockdiag, gh[i][:, LANES:], 0.0)))
    return out


def _rwkv_kernel(r_ref, k_ref, v_ref, lo_ref, z_ref, mur_ref, muk_ref, muv_ref, mulo_ref,
                 wup_ref, aup_ref, w0_ref, a0_ref, kk_ref, ka_ref, rk_ref, gg_ref, gb_ref,
                 o_ref, prev_ref, sbd_ref):
    nb, t, w = r_ref.shape
    c = RWKV_CHUNK
    n = RWKV_HEAD_DIM
    n_pairs = w // LANES
    n_chunks = t // c

    @pl.when(pl.program_id(1) == 0)
    def _():
        prev_ref[...] = jnp.zeros_like(prev_ref)
        sbd_ref[...] = jnp.zeros_like(sbd_ref)

    first_row = lax.broadcasted_iota(jnp.int32, (t, 1), 0) == 0
    wr = lax.broadcasted_iota(jnp.int32, (w, w), 0)
    wc = lax.broadcasted_iota(jnp.int32, (w, w), 1)
    head_ones = ((wr // n) == (wc // n)).astype(BF16)
    tr = lax.broadcasted_iota(jnp.int32, (t, t), 0)
    tc = lax.broadcasted_iota(jnp.int32, (t, t), 1)
    chunk_tri = ((tc <= tr) & ((tc // c) == (tr // c))).astype(BF16)

    def head_sum(x):
        return _bdot(x, head_ones)

    def token_shift(x, mu, bi, lo, hi):
        prev = jnp.where(first_row, prev_ref[bi:bi + 1, lo:hi], pltpu.roll(x, 1, axis=0))
        prev_ref[bi:bi + 1, lo:hi] = x[t - 1:t, :]
        return x + (prev - x) * mu

    prepped = []
    for bi in range(nb):
        r = token_shift(r_ref[bi], mur_ref[...], bi, 0, w)
        k = token_shift(k_ref[bi], muk_ref[...], bi, w, 2 * w)
        v = token_shift(v_ref[bi], muv_ref[...], bi, 2 * w, 3 * w)
        lora = token_shift(lo_ref[bi], mulo_ref[...], bi, 3 * w, 3 * w + lo_ref.shape[2])
        dec = _bdot(jnp.tanh(lora), wup_ref[...])
        wx = -(w0_ref[...] + dec)
        w_raw = -(jnp.maximum(wx, 0.0) + jnp.log1p(jnp.exp(-jnp.abs(wx)))) - 0.5
        logw = -jnp.exp(w_raw)
        a = 1.0 / (1.0 + jnp.exp(-(a0_ref[...] + _bdot(lora, aup_ref[...]))))
        kk = k * kk_ref[...]
        kk = kk / jnp.maximum(jnp.sqrt(head_sum(kk * kk)), 1e-12)
        k2 = k * (1.0 + (a - 1.0) * ka_ref[...])
        prepped.append(dict(r=r, k2=k2, v=v, av=-kk, bv=kk * a, logw=logw, lg=_split2_dot(chunk_tri, logw)))

    lane = lax.broadcasted_iota(jnp.int32, (c, LANES), 1)
    row = lax.broadcasted_iota(jnp.int32, (c, LANES), 0)
    head0 = lane < n
    col = jnp.where(head0, lane, lane - n)
    lane_r = lax.broadcasted_iota(jnp.int32, (LANES, LANES), 0)
    lane_c = lax.broadcasted_iota(jnp.int32, (LANES, LANES), 1)
    masks = (head0, col < row, col == row, (lane_r < n) == (lane_c < n), (lane_r == lane_c).astype(F32))

    seqs = [(bi, pi) for bi in range(nb) for pi in range(n_pairs)]
    units = [{name: arr[ci * c:(ci + 1) * c, pi * LANES:(pi + 1) * LANES] for name, arr in prepped[bi].items()}
             for ci in range(n_chunks) for (bi, pi) in seqs]
    affine = _rwkv_chunks(units, masks)

    sbd = [sbd_ref[si] for si in range(len(seqs))]
    ys = {}
    for ci in range(n_chunks):
        qs = [affine[ci * len(seqs) + si] for si in range(len(seqs))]
        both = [_bdot(jnp.concatenate([qs[si][0], qs[si][2]], axis=0), sbd[si]) for si in range(len(seqs))]
        sbd = [both[si][c:] + qs[si][3] for si in range(len(seqs))]
        for si, seq in enumerate(seqs):
            ys[seq + (ci,)] = both[si][:c] + qs[si][1]
    for si in range(len(seqs)):
        sbd_ref[si] = sbd[si]

    for bi in range(nb):
        y = jnp.concatenate([jnp.concatenate([ys[(bi, pi, ci)] for pi in range(n_pairs)], axis=1)
                             for ci in range(n_chunks)], axis=0)
        p = prepped[bi]
        mu = head_sum(y) * (1.0 / n)
        d = y - mu
        var = head_sum(d * d) * (1.0 / n)
        y = d * lax.rsqrt(var + GN_EPS) * gg_ref[...] + gb_ref[...]
        y = y + head_sum(p["r"] * p["k2"] * rk_ref[...]) * p["v"]
        z = z_ref[bi].astype(F32)
        o_ref[bi] = (y * (z / (1.0 + jnp.exp(-z)))).astype(o_ref.dtype)


def _rwkv_mix(rkvl, qkvz, rows, wup_pad, aup_pad, n_pairs, lora_cols, z_col0, t, pairs_per_step):
    b, s, _ = rkvl.shape
    t = min(t, s)
    w = pairs_per_step * LANES
    p = n_pairs // pairs_per_step
    assert p * pairs_per_step == n_pairs and z_col0 % w == 0
    lora_block = 3 * n_pairs * LANES // lora_cols
    assert lora_block * lora_cols == 3 * n_pairs * LANES
    tok = lambda off: pl.BlockSpec((b, t, w), lambda j, i: (0, i, off + j))
    prow = lambda off: pl.BlockSpec((1, w), lambda j, i: (0, off + j))
    in_specs = [
        tok(0), tok(p), tok(2 * p),
        pl.BlockSpec((b, t, lora_cols), lambda j, i: (0, i, lora_block)),
        tok(z_col0 // w),
        prow(0), prow(p), prow(2 * p),
        pl.BlockSpec((1, lora_cols), lambda j, i: (0, lora_block)),
        pl.BlockSpec((lora_cols, w), lambda j, i: (0, j)),
        pl.BlockSpec((lora_cols, w), lambda j, i: (0, j)),
    ] + [prow(0)] * 7
    return pl.pallas_call(
        _rwkv_kernel,
        grid=(p, s // t),
        in_specs=in_specs,
        out_specs=pl.BlockSpec((b, t, w), lambda j, i: (0, i, j)),
        out_shape=jax.ShapeDtypeStruct((b, s, n_pairs * LANES), BF16),
        name="rwkv7_mix",
        scratch_shapes=[pltpu.VMEM((8, 3 * w + lora_cols), F32),
                        pltpu.VMEM((b * pairs_per_step, LANES, LANES), F32)],
        compiler_params=pltpu.CompilerParams(
            dimension_semantics=("parallel", "arbitrary"), vmem_limit_bytes=VMEM_LIMIT_BYTES),
    )(rkvl, rkvl, rkvl, rkvl, qkvz, rows["mu"], rows["mu"], rows["mu"], rows["mu"], wup_pad, aup_pad,
      rows["w0"], rows["a0"], rows["k_k"], rows["k_a"], rows["r_k"], rows["gn_gain"], rows["gn_bias"])


def _out_kernel(hf_ref, hr_ref, wt_ref, wb_ref, x_ref, g_ref, b_ref, o_ref, acc_ref, sum_ref, *, alpha, n_tiles):
    j = pl.program_id(1)
    out = (jnp.dot(hf_ref[...], wt_ref[...], preferred_element_type=F32)
           + jnp.dot(hr_ref[...], wb_ref[...], preferred_element_type=F32))
    pre = alpha * x_ref[...] + out
    acc_ref[j] = pre
    part = jnp.sum(pre, axis=-1, keepdims=True)
    sum_ref[...] = jnp.where(j == 0, part, sum_ref[...] + part)

    @pl.when(j == n_tiles - 1)
    def _():
        tn = acc_ref.shape[2]
        d = n_tiles * tn
        mean = sum_ref[...] * (1.0 / d)
        sq = jnp.sum(jnp.square(acc_ref[0] - mean), axis=-1, keepdims=True)
        for jj in range(1, n_tiles):
            sq = sq + jnp.sum(jnp.square(acc_ref[jj] - mean), axis=-1, keepdims=True)
        inv = lax.rsqrt(sq * (1.0 / d) + LN_EPS)
        for jj in range(n_tiles):
            sl = slice(jj * tn, (jj + 1) * tn)
            o_ref[:, sl] = ((acc_ref[jj] - mean) * inv * g_ref[:, sl] + b_ref[:, sl]).astype(o_ref.dtype)


def _out_proj_layernorm(hf, hr, w_out_bf16, x2, ln_gain, ln_bias, alpha, tm, tn):
    m, d = x2.shape
    kf, kr = hf.shape[1], hr.shape[1]
    assert kf == kr and kf + kr == w_out_bf16.shape[0]
    tm, tn = min(tm, m), min(tn, d)
    n_tiles = d // tn
    return pl.pallas_call(
        functools.partial(_out_kernel, alpha=alpha, n_tiles=n_tiles),
        grid=(m // tm, n_tiles),
        in_specs=[pl.BlockSpec((tm, kf), lambda i, j: (i, 0)),
                  pl.BlockSpec((tm, kr), lambda i, j: (i, 0)),
                  pl.BlockSpec((kf, tn), lambda i, j: (0, j)),
                  pl.BlockSpec((kr, tn), lambda i, j: (1, j)),
                  pl.BlockSpec((tm, tn), lambda i, j: (i, j)),
                  pl.BlockSpec((1, d), lambda i, j: (0, 0)),
                  pl.BlockSpec((1, d), lambda i, j: (0, 0))],
        out_specs=pl.BlockSpec((tm, d), lambda i, j: (i, 0)),
        out_shape=jax.ShapeDtypeStruct((m, d), x2.dtype),
        name="out_proj_layernorm",
        scratch_shapes=[pltpu.VMEM((n_tiles, tm, tn), F32), pltpu.VMEM((tm, 1), F32)],
        compiler_params=pltpu.CompilerParams(
            dimension_semantics=("parallel", "arbitrary"), vmem_limit_bytes=VMEM_LIMIT_BYTES),
    )(hf, hr, w_out_bf16, w_out_bf16, x2, ln_gain.reshape(1, d).astype(F32), ln_bias.reshape(1, d).astype(F32))


def kernel(x, w_in, f_bias, mu_shift, w0, w_up, a0, a_up, k_k, k_a, r_k, gn_gain, gn_bias, w_out, ln_gain, ln_bias):
    b, s, d = x.shape
    fox_heads = f_bias.shape[0]
    fw = fox_heads * FOX_HEAD_DIM
    rw = w0.shape[0]
    n_pairs = rw // LANES
    assert r_k.shape[1] == RWKV_HEAD_DIM and rw % LANES == 0
    dl, al = w_up.shape[0], a_up.shape[0]
    mix = fw + rw
    assert w_in.shape[1] == 3 * fw + fox_heads + 3 * rw + dl + al + mix
    depth = 1
    alpha = (2 * depth) ** 0.25

    o_ff = 3 * fw
    o_r = o_ff + fox_heads
    o_wd = o_r + 3 * rw
    o_ad = o_wd + dl
    o_z = o_ad + al

    lora_used = dl + al + fox_heads
    lora_cols = -(-lora_used // 256) * 256
    assert (3 * rw) % lora_cols == 0 and (dl + al) % LANES + fox_heads <= LANES
    gate_block = (dl + al) // LANES
    gate_off = (dl + al) % LANES

    xb = x.reshape(b * s, d).astype(BF16)
    w_in_bf = w_in.astype(BF16)
    w_bf = jnp.concatenate([w_in_bf[:, :3 * fw], w_in_bf[:, o_z:]], axis=1)
    w_f32 = jnp.concatenate(
        [w_in_bf[:, o_r:o_ad + al], w_in_bf[:, o_ff:o_r], jnp.zeros((d, lora_cols - lora_used), BF16)],
        axis=1)
    q_scale = jnp.concatenate([jnp.full((1, fw), LOG2_E * FOX_HEAD_DIM ** -0.5, F32),
                               jnp.ones((1, 2 * fw + mix), F32)], axis=1)

    qkvz = _matmul(xb, w_bf, BF16, 1024, 1024, col_scale=q_scale).reshape(b, s, 3 * fw + mix)
    rkvl = _matmul(xb, w_f32, F32, 512, 1280).reshape(b, s, 3 * rw + lora_cols)

    bias_row = jnp.zeros((1, LANES), F32).at[0, gate_off:gate_off + fox_heads].set(f_bias.astype(F32))
    c2 = _gate_cumsum(rkvl, bias_row, 3 * rw // LANES + gate_block, 512)
    hf = _fox_attention(qkvz, c2, fox_heads, min(512, s), gate_off, 2)

    pad_rows = lambda w, off: jnp.zeros((lora_cols, rw), F32).at[off:off + w.shape[0]].set(w.astype(F32)).astype(BF16)
    mu_row = jnp.concatenate([mu_shift.astype(F32), jnp.zeros((lora_cols - dl - al,), F32)]).reshape(1, -1)
    rows = {"mu": mu_row}
    for name, val in (("w0", w0), ("a0", a0), ("k_k", k_k), ("k_a", k_a), ("r_k", r_k),
                      ("gn_gain", gn_gain), ("gn_bias", gn_bias)):
        rows[name] = val.astype(F32).reshape(1, rw)
    hr = _rwkv_mix(rkvl, qkvz, rows, pad_rows(w_up, 0), pad_rows(a_up, dl), n_pairs, lora_cols,
                   3 * fw + fw, 256, 2)

    out = _out_proj_layernorm(hf.reshape(b * s, fw), hr.reshape(b * s, rw), w_out.astype(BF16),
                              x.reshape(b * s, d), ln_gain, ln_bias, alpha, 512, 512)
    return out.reshape(b, s, d)
```

```python
import functools
import math

import jax
import jax.numpy as jnp
from jax import lax
from jax.experimental import pallas as pl
from jax.experimental.pallas import tpu as pltpu

F32 = jnp.float32
BF16 = jnp.bfloat16

LANES = 128
BF16_SUBLANES = 16
FOX_HEAD_DIM = 128
RWKV_HEAD_DIM = 64
RWKV_CHUNK = 64
LOG2_E = 1.4426950408889634
EXP_NEG_HALF = 0.6065306597126334
LN_EPS = 1e-5
GN_EPS = 64e-5
VMEM_LIMIT_BYTES = 56 * 1024 * 1024


def _bdot(a, b, dims=(((1,), (0,)), ((), ()))):
    return lax.dot_general(a.astype(BF16), b.astype(BF16), dims, preferred_element_type=F32)


def _split3_dot(m_bf16, x):
    hi = x.astype(BF16)
    r1 = x - hi.astype(F32)
    mid = r1.astype(BF16)
    lo = (r1 - mid.astype(F32)).astype(BF16)
    dot = functools.partial(jnp.dot, preferred_element_type=F32)
    return dot(m_bf16, hi) + dot(m_bf16, mid) + dot(m_bf16, lo)


def _split2_dot(m_bf16, x):
    hi = x.astype(BF16)
    lo = (x - hi.astype(F32)).astype(BF16)
    dot = functools.partial(jnp.dot, preferred_element_type=F32)
    return dot(m_bf16, hi) + dot(m_bf16, lo)


def _pick_tile(n, preferred):
    t = min(preferred, n) // LANES * LANES
    while n % t:
        t -= LANES
    return t


_NT = (((1,), (1,)), ((), ()))


def _proj_fox_kernel(a_ref, wt_ref, o_ref, a16_ref, *, n_q_blocks, n_qkv_blocks, q_scale):
    j = pl.program_id(1)

    @pl.when(j == 0)
    def _():
        a16_ref[...] = a_ref[...].astype(BF16)

    acc = lax.dot_general(a16_ref[...], wt_ref[...], _NT, preferred_element_type=F32)
    sigmoid = 1.0 / (1.0 + jnp.exp(-acc))
    factor = jnp.where(j >= n_qkv_blocks, sigmoid, jnp.where(j < n_q_blocks, q_scale, 1.0))
    o_ref[...] = (acc * factor).astype(o_ref.dtype)


def _proj_rwkv_kernel(a_ref, wt_ref, et_ref, mu_ref, o_ref, e_ref, a16_ref, last_ref, *, blocks_per_seq):
    i, j = pl.program_id(0), pl.program_id(1)

    @pl.when(j == 0)
    def _():
        a16_ref[...] = a_ref[...].astype(BF16)
        e_ref[...] = lax.dot_general(a16_ref[...], et_ref[...], _NT, preferred_element_type=F32)

    p = lax.dot_general(a16_ref[...], wt_ref[...], _NT, preferred_element_type=F32)
    tm = p.shape[0]
    above = jnp.where(i % blocks_per_seq == 0, 0.0, last_ref[j])
    first_row = lax.broadcasted_iota(jnp.int32, (tm, 1), 0) == 0
    prev = jnp.where(first_row, above, pltpu.roll(p, 1, axis=0))
    last_ref[j] = p[tm - 1:tm, :]
    o_ref[...] = p + (prev - p) * mu_ref[...]


def _w_row_map(segments, tn):
    def w_row(j):
        row, first = None, 0
        for nb, row0 in segments:
            here = row0 + (j - first) * tn
            row = here if row is None else jnp.where(j >= first, here, row)
            first += nb
        return pl.multiple_of(row, BF16_SUBLANES)
    return w_row


def _in_proj_fox(a, w_t, tm, tn, n_q_blocks, n_qkv_blocks, z_row0, n_z_blocks, q_scale):
    m, k = a.shape
    tm = _pick_tile(m, tm)
    w_row = _w_row_map([(n_qkv_blocks, 0), (n_z_blocks, z_row0)], tn)
    n_blocks = n_qkv_blocks + n_z_blocks
    return pl.pallas_call(
        functools.partial(_proj_fox_kernel, n_q_blocks=n_q_blocks, n_qkv_blocks=n_qkv_blocks, q_scale=q_scale),
        grid=(m // tm, n_blocks),
        in_specs=[pl.BlockSpec((tm, k), lambda i, j: (i, 0)),
                  pl.BlockSpec((pl.Element(tn), pl.Element(k)), lambda i, j: (w_row(j), 0))],
        out_specs=pl.BlockSpec((tm, tn), lambda i, j: (i, j)),
        out_shape=jax.ShapeDtypeStruct((m, n_blocks * tn), BF16),
        name="in_proj_fox",
        scratch_shapes=[pltpu.VMEM((tm, k), BF16)],
        compiler_params=pltpu.CompilerParams(
            dimension_semantics=("parallel", "arbitrary"), vmem_limit_bytes=VMEM_LIMIT_BYTES),
    )(a, w_t)


def _in_proj_rwkv(a, w_t, mu_row, tm, tn, row0, n_blocks, gate_row0, seq_len):
    m, k = a.shape
    tm = _pick_tile(math.gcd(m, seq_len), tm)
    w_row = _w_row_map([(n_blocks, row0)], tn)
    return pl.pallas_call(
        functools.partial(_proj_rwkv_kernel, blocks_per_seq=seq_len // tm),
        grid=(m // tm, n_blocks),
        in_specs=[pl.BlockSpec((tm, k), lambda i, j: (i, 0)),
                  pl.BlockSpec((pl.Element(tn), pl.Element(k)), lambda i, j: (w_row(j), 0)),
                  pl.BlockSpec((pl.Element(LANES), pl.Element(k)), lambda i, j: (gate_row0, 0)),
                  pl.BlockSpec((1, tn), lambda i, j: (0, j))],
        out_specs=[pl.BlockSpec((tm, tn), lambda i, j: (i, j)),
                   pl.BlockSpec((tm, LANES), lambda i, j: (i, 0))],
        out_shape=[jax.ShapeDtypeStruct((m, n_blocks * tn), F32), jax.ShapeDtypeStruct((m, LANES), F32)],
        name="in_proj_rwkv",
        scratch_shapes=[pltpu.VMEM((tm, k), BF16), pltpu.VMEM((n_blocks, 1, tn), F32)],
        compiler_params=pltpu.CompilerParams(
            dimension_semantics=("arbitrary", "arbitrary"), vmem_limit_bytes=VMEM_LIMIT_BYTES),
    )(a, w_t, w_t, mu_row)


def _gate_kernel(x_ref, b_ref, o_ref, carry_ref):
    @pl.when(pl.program_id(1) == 0)
    def _():
        carry_ref[...] = jnp.zeros_like(carry_ref)

    t = x_ref.shape[1]
    x = x_ref[0] + b_ref[...]
    logf = jnp.minimum(x, 0.0) - jnp.log1p(jnp.exp(-jnp.abs(x)))
    tri = (lax.broadcasted_iota(jnp.int32, (t, t), 1)
           <= lax.broadcasted_iota(jnp.int32, (t, t), 0)).astype(BF16)
    c = _split3_dot(tri, logf) + carry_ref[0:1, :]
    o_ref[0] = c * LOG2_E
    carry_ref[0:1, :] = c[t - 1:t, :]


def _gate_cumsum(lora3, bias_row, col_block, t):
    b, s, _ = lora3.shape
    t = min(t, s)
    return pl.pallas_call(
        _gate_kernel,
        grid=(b, s // t),
        in_specs=[pl.BlockSpec((1, t, LANES), lambda bi, i: (bi, i, col_block)),
                  pl.BlockSpec((1, LANES), lambda bi, i: (0, 0))],
        out_specs=pl.BlockSpec((1, t, LANES), lambda bi, i: (bi, i, 0)),
        out_shape=jax.ShapeDtypeStruct((b, s, LANES), F32),
        name="fox_gate_cumsum",
        scratch_shapes=[pltpu.VMEM((8, LANES), F32)],
        compiler_params=pltpu.CompilerParams(dimension_semantics=("parallel", "arbitrary")),
    )(lora3, bias_row)


def _fox_kernel(q_ref, k_ref, v_ref, c_ref, z_ref, o_ref, kaug_ref, vt_ref, sa_ref, sb_ref, pa_ref, pb_ref,
                m_ref, al_ref, acc_ref, *, tq, c_lane0, group_cols):
    i = pl.program_id(2)
    s_len = k_ref.shape[1]
    dh = FOX_HEAD_DIM
    n_heads = k_ref.shape[2] // dh
    gw = group_cols
    n_groups = tq // gw
    units = [(hh, g) for hh in range(n_heads) for g in range(n_groups)]
    cols = [slice(u * gw, (u + 1) * gw) for u in range(len(units))]

    @pl.when(i == 0)
    def _():
        lane_r = lax.broadcasted_iota(jnp.int32, (LANES, LANES), 0)
        lane_c = lax.broadcasted_iota(jnp.int32, (LANES, LANES), 1)
        dot = functools.partial(jnp.dot, preferred_element_type=F32)
        for hh in range(n_heads):
            c_lane = c_lane0 + pl.program_id(1) * n_heads + hh
            pick = [jnp.where((lane_r == c_lane) & (lane_c == term), 1.0, 0.0).astype(BF16) for term in range(3)]
            hs = slice(hh * dh, (hh + 1) * dh)
            for jb in range(s_len // tq):
                sl = slice(jb * tq, (jb + 1) * tq)
                c = c_ref[0, sl, :]
                hi = c.astype(BF16)
                r1 = c - hi.astype(F32)
                mid = r1.astype(BF16)
                lo = (r1 - mid.astype(F32)).astype(BF16)
                c_terms = dot(hi, pick[0]) + dot(mid, pick[1]) + dot(lo, pick[2])
                kaug_ref[hh, sl, :dh] = k_ref[0, sl, hs]
                kaug_ref[hh, sl, dh:] = (-c_terms).astype(BF16)
                vt_ref[hh, jb, :dh, :] = v_ref[0, sl, hs].astype(F32).T.astype(BF16)
                vt_ref[hh, jb, dh:, :] = jnp.ones((vt_ref.shape[2] - dh, tq), BF16)

    ones3 = jnp.where(lax.broadcasted_iota(jnp.int32, (dh, tq), 0) < 3, 1.0, 0.0)
    q_t = [jnp.concatenate([q_ref[0, :, hh * dh:(hh + 1) * dh].astype(F32).T, ones3], axis=0).astype(BF16)
           for hh in range(n_heads)]
    q_u = [q_t[hh][:, g * gw:(g + 1) * gw] for (hh, g) in units]

    def scores(j, s_out):
        rows = pl.ds(pl.multiple_of(j * tq, tq), tq)
        for u, (hh, _) in enumerate(units):
            s_out[:, cols[u]] = jnp.dot(kaug_ref[hh, rows, :], q_u[u], preferred_element_type=F32)

    def accumulate(p, j):
        for u, (hh, _) in enumerate(units):
            acc_ref[:, cols[u]] = (al_ref[:, cols[u]] * acc_ref[:, cols[u]]
                                   + jnp.dot(vt_ref[hh, j], p[u], preferred_element_type=F32))

    def softmax_step(s):
        p = []
        for u in range(len(units)):
            m = m_ref[:, cols[u]]
            m_new = jnp.maximum(m, jnp.max(s[u], axis=0, keepdims=True))
            p.append(jnp.exp2(s[u] - m_new).astype(BF16))
            al_ref[:, cols[u]] = jnp.exp2(m - m_new)
            m_ref[:, cols[u]] = m_new
        return p

    def step(j, s_in, s_out, p_in, p_out):
        scores(j + 1, s_out)
        accumulate([p_in[:, cs] for cs in cols], jnp.maximum(j - 1, 0))
        p = softmax_step([s_in[:, cs] for cs in cols])
        for u, cs in enumerate(cols):
            p_out[:, cs] = p[u]

    scores(0, sa_ref)
    pa_ref[...] = jnp.zeros_like(pa_ref)
    m_ref[...] = jnp.full(m_ref.shape, -jnp.inf, F32)
    al_ref[...] = jnp.ones_like(al_ref)
    acc_ref[...] = jnp.zeros_like(acc_ref)

    def pair(t, carry):
        step(2 * t, sa_ref, sb_ref, pa_ref, pb_ref)
        step(2 * t + 1, sb_ref, sa_ref, pb_ref, pa_ref)
        return carry

    lax.fori_loop(0, i // 2, pair, 0)

    @pl.when(i % 2 == 1)
    def _():
        step(i - 1, sa_ref, sb_ref, pa_ref, pb_ref)
        sa_ref[...] = sb_ref[...]
        pa_ref[...] = pb_ref[...]

    accumulate([pa_ref[:, cs] for cs in cols], jnp.maximum(i - 1, 0))
    key = lax.broadcasted_iota(jnp.int32, (tq, gw), 0)
    qry = lax.broadcasted_iota(jnp.int32, (tq, gw), 1)
    p = softmax_step([jnp.where(key <= qry + g * gw, sa_ref[:, cols[u]], -jnp.inf)
                      for u, (_, g) in enumerate(units)])
    accumulate(p, i)
    for u, (hh, g) in enumerate(units):
        acc_u = acc_ref[:, cols[u]]
        gate = z_ref[0, g * gw:(g + 1) * gw, hh * dh:(hh + 1) * dh].astype(F32)
        out = (acc_u[:dh, :] / acc_u[dh:dh + 1, :]).T
        o_ref[0, g * gw:(g + 1) * gw, hh * dh:(hh + 1) * dh] = (out * gate).astype(o_ref.dtype)


def _fox_attention(qkvz, c2, n_heads, tq, c_lane0, heads_per_step):
    b, s, _ = qkvz.shape
    nq = s // tq
    nh = heads_per_step
    hb = n_heads // nh
    assert hb * nh == n_heads
    w = nh * FOX_HEAD_DIM
    acc_rows = FOX_HEAD_DIM + BF16_SUBLANES
    return pl.pallas_call(
        functools.partial(_fox_kernel, tq=tq, c_lane0=c_lane0, group_cols=min(256, tq)),
        grid=(b, hb, nq),
        in_specs=[pl.BlockSpec((1, tq, w), lambda bi, hi, i: (bi, i, hi)),
                  pl.BlockSpec((1, s, w), lambda bi, hi, i: (bi, 0, hb + hi)),
                  pl.BlockSpec((1, s, w), lambda bi, hi, i: (bi, 0, 2 * hb + hi)),
                  pl.BlockSpec((1, s, LANES), lambda bi, hi, i: (bi, 0, 0)),
                  pl.BlockSpec((1, tq, w), lambda bi, hi, i: (bi, i, 3 * hb + hi))],
        out_specs=pl.BlockSpec((1, tq, w), lambda bi, hi, i: (bi, i, hi)),
        out_shape=jax.ShapeDtypeStruct((b, s, n_heads * FOX_HEAD_DIM), BF16),
        name="fox_attention",
        scratch_shapes=[pltpu.VMEM((nh, s, 2 * FOX_HEAD_DIM), BF16),
                        pltpu.VMEM((nh, nq, acc_rows, tq), BF16),
                        pltpu.VMEM((tq, nh * tq), F32), pltpu.VMEM((tq, nh * tq), F32),
                        pltpu.VMEM((tq, nh * tq), BF16), pltpu.VMEM((tq, nh * tq), BF16),
                        pltpu.VMEM((1, nh * tq), F32), pltpu.VMEM((1, nh * tq), F32),
                        pltpu.VMEM((acc_rows, nh * tq), F32)],
        compiler_params=pltpu.CompilerParams(
            dimension_semantics=("parallel", "parallel", "arbitrary"), vmem_limit_bytes=VMEM_LIMIT_BYTES),
    )(qkvz, qkvz, qkvz, c2, qkvz)


def _stack_heads(x, head0):
    zero = jnp.zeros_like(x)
    return jnp.concatenate([jnp.where(head0, x, zero), jnp.where(head0, zero, x)], axis=0)


def _rwkv_chunks(units, masks):
    head0, strict, diag, blockdiag, eye = masks
    incl = strict | diag
    c = units[0]["r"].shape[0]
    n_units = range(len(units))
    zero = jnp.zeros((c, LANES), F32)
    nt = (((1,), (1,)), ((), ()))
    tn = (((0,), (0,)), ((), ()))

    rt, at, bh, kh, gc, sv, lhs1, rhs1 = [], [], [], [], [], [], [], []
    for u in units:
        lg = u["lg"]
        lgc = lg[c - 1:c, :]
        e_neg = jnp.exp(-lg)
        e_end = jnp.exp(lgc - lg)
        rt.append(u["r"] * jnp.exp(lg))
        at.append(u["av"] * jnp.exp(lg - u["logw"]))
        bh.append(u["bv"] * e_end)
        kh.append(u["k2"] * e_end)
        gc.append(jnp.exp(lgc))
        sv.append(_stack_heads(u["v"], head0))
        lhs1.append(jnp.concatenate([at[-1], rt[-1]], axis=0))
        rhs1.append(jnp.concatenate([_stack_heads(u["bv"] * e_neg, head0),
                                     _stack_heads(u["k2"] * e_neg, head0)], axis=0))

    big = [_bdot(lhs1[i], rhs1[i], nt) for i in n_units]
    l_mat = [jnp.where(strict, big[i][:c, :LANES], zero) for i in n_units]
    a_ak = [jnp.where(strict, big[i][:c, LANES:], zero) for i in n_units]
    a_r = [jnp.concatenate([jnp.where(incl, big[i][c:, :LANES], zero),
                            jnp.where(incl, big[i][c:, LANES:], zero)], axis=1) for i in n_units]

    aakv = [_bdot(a_ak[i], sv[i]) for i in n_units]
    n_steps = max(1, (c - 1).bit_length())
    t_inv = [jnp.where(diag, 1.0, 0.0) + l_mat[i] for i in n_units]
    cur = [_bdot(l_mat[i], _stack_heads(l_mat[i], head0)) for i in n_units] if n_steps > 1 else None
    for step in range(1, n_steps):
        if step + 1 < n_steps:
            both = [_bdot(jnp.concatenate([cur[i], t_inv[i]], axis=0), _stack_heads(cur[i], head0)) for i in n_units]
            t_inv = [t_inv[i] + both[i][c:] for i in n_units]
            cur = [both[i][:c] for i in n_units]
        else:
            t_inv = [t_inv[i] + _bdot(t_inv[i], _stack_heads(cur[i], head0)) for i in n_units]
    x = [_bdot(t_inv[i], jnp.concatenate([_stack_heads(at[i], head0), _stack_heads(aakv[i], head0)], axis=1))
         for i in n_units]

    rhs2 = [jnp.concatenate(
        [jnp.concatenate([_stack_heads(x[i][:, :LANES], head0), _stack_heads(x[i][:, LANES:], head0)], axis=1),
         jnp.concatenate([jnp.zeros((2 * c, LANES), F32), sv[i]], axis=1)], axis=0) for i in n_units]
    qy = [_bdot(a_r[i], rhs2[i]) for i in n_units]
    gh = [_bdot(jnp.concatenate([bh[i], kh[i]], axis=0),
                jnp.concatenate([x[i], jnp.concatenate([zero, units[i]["v"]], axis=1)], axis=0), tn)
          for i in n_units]
    out = []
    for i in n_units:
        out.append((rt[i] + qy[i][:, :LANES], qy[i][:, LANES:],
                    jnp.where(blockdiag, gh[i][:, :LANES], 0.0) + eye * gc[i],
                    jnp.where(blockdiag, gh[i][:, LANES:], 0.0)))
    return out


def _rwkv_kernel(r_ref, k_ref, v_ref, lo_ref, z_ref, wup_ref, aup_ref, w0_ref, a0_ref, kk_ref, ka_ref, rk_ref,
                 gg_ref, gb_ref, o_ref, sbd_ref):
    nb, t, w = r_ref.shape
    c = RWKV_CHUNK
    n = RWKV_HEAD_DIM
    n_pairs = w // LANES
    n_chunks = t // c

    @pl.when(pl.program_id(1) == 0)
    def _():
        sbd_ref[...] = jnp.zeros_like(sbd_ref)

    wr = lax.broadcasted_iota(jnp.int32, (w, w), 0)
    wc = lax.broadcasted_iota(jnp.int32, (w, w), 1)
    head_ones = ((wr // n) == (wc // n)).astype(BF16)
    tr = lax.broadcasted_iota(jnp.int32, (t, t), 0)
    tc = lax.broadcasted_iota(jnp.int32, (t, t), 1)
    chunk_tri = ((tc <= tr) & ((tc // c) == (tr // c))).astype(BF16)

    def head_sum(x):
        return _bdot(x, head_ones)

    prepped = []
    for bi in range(nb):
        r, k, v, lora = r_ref[bi], k_ref[bi], v_ref[bi], lo_ref[bi]
        dec = _bdot(jnp.tanh(lora), wup_ref[...])
        logw = -EXP_NEG_HALF / (1.0 + jnp.exp(-(w0_ref[...] + dec)))
        a = 1.0 / (1.0 + jnp.exp(-(a0_ref[...] + _bdot(lora, aup_ref[...]))))
        kk = k * kk_ref[...]
        kk = kk * lax.rsqrt(jnp.maximum(head_sum(kk * kk), 1e-24))
        k2 = k * (1.0 + (a - 1.0) * ka_ref[...])
        prepped.append(dict(r=r, k2=k2, v=v, av=-kk, bv=kk * a, logw=logw, lg=_split2_dot(chunk_tri, logw)))

    lane = lax.broadcasted_iota(jnp.int32, (c, LANES), 1)
    row = lax.broadcasted_iota(jnp.int32, (c, LANES), 0)
    head0 = lane < n
    col = jnp.where(head0, lane, lane - n)
    lane_r = lax.broadcasted_iota(jnp.int32, (LANES, LANES), 0)
    lane_c = lax.broadcasted_iota(jnp.int32, (LANES, LANES), 1)
    masks = (head0, col < row, col == row, (lane_r < n) == (lane_c < n), (lane_r == lane_c).astype(F32))

    seqs = [(bi, pi) for bi in range(nb) for pi in range(n_pairs)]
    units = [{name: arr[ci * c:(ci + 1) * c, pi * LANES:(pi + 1) * LANES] for name, arr in prepped[bi].items()}
             for ci in range(n_chunks) for (bi, pi) in seqs]
    affine = _rwkv_chunks(units, masks)

    sbd = [sbd_ref[si] for si in range(len(seqs))]
    ys = {}
    for ci in range(n_chunks):
        qs = [affine[ci * len(seqs) + si] for si in range(len(seqs))]
        both = [_bdot(jnp.concatenate([qs[si][0], qs[si][2]], axis=0), sbd[si]) for si in range(len(seqs))]
        sbd = [both[si][c:] + qs[si][3] for si in range(len(seqs))]
        for si, seq in enumerate(seqs):
            ys[seq + (ci,)] = both[si][:c] + qs[si][1]
    for si in range(len(seqs)):
        sbd_ref[si] = sbd[si]

    for bi in range(nb):
        y = jnp.concatenate([jnp.concatenate([ys[(bi, pi, ci)] for pi in range(n_pairs)], axis=1)
                             for ci in range(n_chunks)], axis=0)
        p = prepped[bi]
        mu = head_sum(y) * (1.0 / n)
        d = y - mu
        var = head_sum(d * d) * (1.0 / n)
        y = d * lax.rsqrt(var + GN_EPS) * gg_ref[...] + gb_ref[...]
        y = y + head_sum(p["r"] * p["k2"] * rk_ref[...]) * p["v"]
        o_ref[bi] = (y * z_ref[bi].astype(F32)).astype(o_ref.dtype)


def _rwkv_mix(rkvl, qkvz, rows, wup_pad, aup_pad, n_pairs, lora_cols, z_col0, t, pairs_per_step):
    b, s, _ = rkvl.shape
    t = min(t, s)
    w = pairs_per_step * LANES
    p = n_pairs // pairs_per_step
    assert p * pairs_per_step == n_pairs and z_col0 % w == 0
    lora_block = 3 * n_pairs * LANES // lora_cols
    assert lora_block * lora_cols == 3 * n_pairs * LANES
    tok = lambda off: pl.BlockSpec((b, t, w), lambda j, i: (0, i, off + j))
    prow = lambda off: pl.BlockSpec((1, w), lambda j, i: (0, off + j))
    in_specs = [
        tok(0), tok(p), tok(2 * p),
        pl.BlockSpec((b, t, lora_cols), lambda j, i: (0, i, lora_block)),
        tok(z_col0 // w),
        pl.BlockSpec((lora_cols, w), lambda j, i: (0, j)),
        pl.BlockSpec((lora_cols, w), lambda j, i: (0, j)),
    ] + [prow(0)] * 7
    return pl.pallas_call(
        _rwkv_kernel,
        grid=(p, s // t),
        in_specs=in_specs,
        out_specs=pl.BlockSpec((b, t, w), lambda j, i: (0, i, j)),
        out_shape=jax.ShapeDtypeStruct((b, s, n_pairs * LANES), BF16),
        name="rwkv7_mix",
        scratch_shapes=[pltpu.VMEM((b * pairs_per_step, LANES, LANES), F32)],
        compiler_params=pltpu.CompilerParams(
            dimension_semantics=("parallel", "arbitrary"), vmem_limit_bytes=VMEM_LIMIT_BYTES),
    )(rkvl, rkvl, rkvl, rkvl, qkvz, wup_pad, aup_pad,
      rows["w0"], rows["a0"], rows["k_k"], rows["k_a"], rows["r_k"], rows["gn_gain"], rows["gn_bias"])


def _out_kernel(hf_ref, hr_ref, wt_ref, wb_ref, x_ref, g_ref, b_ref, o_ref, acc_ref, sum_ref, *, alpha, n_tiles):
    j = pl.program_id(1)
    out = (jnp.dot(hf_ref[...], wt_ref[...], preferred_element_type=F32)
           + jnp.dot(hr_ref[...], wb_ref[...], preferred_element_type=F32))
    pre = alpha * x_ref[...] + out
    acc_ref[j] = pre
    part = jnp.sum(pre, axis=-1, keepdims=True)
    sum_ref[...] = jnp.where(j == 0, part, sum_ref[...] + part)

    @pl.when(j == n_tiles - 1)
    def _():
        tn = acc_ref.shape[2]
        d = n_tiles * tn
        mean = sum_ref[...] * (1.0 / d)
        sq = jnp.sum(jnp.square(acc_ref[0] - mean), axis=-1, keepdims=True)
        for jj in range(1, n_tiles):
            sq = sq + jnp.sum(jnp.square(acc_ref[jj] - mean), axis=-1, keepdims=True)
        inv = lax.rsqrt(sq * (1.0 / d) + LN_EPS)
        for jj in range(n_tiles):
            sl = slice(jj * tn, (jj + 1) * tn)
            o_ref[:, sl] = ((acc_ref[jj] - mean) * inv * g_ref[:, sl] + b_ref[:, sl]).astype(o_ref.dtype)


def _out_proj_layernorm(hf, hr, w_out_bf16, x2, ln_gain, ln_bias, alpha, tm, tn):
    m, d = x2.shape
    kf, kr = hf.shape[1], hr.shape[1]
    assert kf == kr and kf + kr == w_out_bf16.shape[0]
    tm, tn = min(tm, m), min(tn, d)
    n_tiles = d // tn
    return pl.pallas_call(
        functools.partial(_out_kernel, alpha=alpha, n_tiles=n_tiles),
        grid=(m // tm, n_tiles),
        in_specs=[pl.BlockSpec((tm, kf), lambda i, j: (i, 0)),
                  pl.BlockSpec((tm, kr), lambda i, j: (i, 0)),
                  pl.BlockSpec((kf, tn), lambda i, j: (0, j)),
                  pl.BlockSpec((kr, tn), lambda i, j: (1, j)),
                  pl.BlockSpec((tm, tn), lambda i, j: (i, j)),
                  pl.BlockSpec((1, d), lambda i, j: (0, 0)),
                  pl.BlockSpec((1, d), lambda i, j: (0, 0))],
        out_specs=pl.BlockSpec((tm, d), lambda i, j: (i, 0)),
        out_shape=jax.ShapeDtypeStruct((m, d), x2.dtype),
        name="out_proj_layernorm",
        scratch_shapes=[pltpu.VMEM((n_tiles, tm, tn), F32), pltpu.VMEM((tm, 1), F32)],
        compiler_params=pltpu.CompilerParams(
            dimension_semantics=("parallel", "arbitrary"), vmem_limit_bytes=VMEM_LIMIT_BYTES),
    )(hf, hr, w_out_bf16, w_out_bf16, x2, ln_gain.reshape(1, d).astype(F32), ln_bias.reshape(1, d).astype(F32))


def kernel(x, w_in, f_bias, mu_shift, w0, w_up, a0, a_up, k_k, k_a, r_k, gn_gain, gn_bias, w_out, ln_gain, ln_bias):
    b, s, d = x.shape
    fox_heads = f_bias.shape[0]
    fw = fox_heads * FOX_HEAD_DIM
    rw = w0.shape[0]
    n_pairs = rw // LANES
    assert r_k.shape[1] == RWKV_HEAD_DIM and rw % LANES == 0
    dl, al = w_up.shape[0], a_up.shape[0]
    mix = fw + rw
    assert w_in.shape[1] == 3 * fw + fox_heads + 3 * rw + dl + al + mix
    depth = 1
    alpha = (2 * depth) ** 0.25

    o_ff = 3 * fw
    o_r = o_ff + fox_heads
    o_wd = o_r + 3 * rw
    o_ad = o_wd + dl
    o_z = o_ad + al

    lora_cols = -(-(dl + al) // 256) * 256
    assert (3 * rw) % lora_cols == 0 and fox_heads <= LANES
    assert o_r + 3 * rw + lora_cols <= w_in.shape[1] and o_ff + LANES <= w_in.shape[1]
    assert all(off % BF16_SUBLANES == 0 for off in (o_ff, o_r, o_z)), "weight row windows must be tile aligned"

    x2 = x.reshape(b * s, d)
    w_t = w_in.T.astype(BF16)
    tn_bf = _pick_tile(math.gcd(fw, mix), 1024)
    tn_f32 = _pick_tile(3 * rw + lora_cols, 1280)
    mu_row = jnp.concatenate([mu_shift.astype(F32), jnp.zeros((lora_cols - dl - al,), F32)]).reshape(1, -1)

    qkvz = _in_proj_fox(x2, w_t, 512, tn_bf, fw // tn_bf, 3 * fw // tn_bf, o_z, mix // tn_bf,
                        LOG2_E * FOX_HEAD_DIM ** -0.5).reshape(b, s, 3 * fw + mix)
    rkvl, ff = _in_proj_rwkv(x2, w_t, mu_row, 512, tn_f32, o_r, (3 * rw + lora_cols) // tn_f32, o_ff, s)
    rkvl = rkvl.reshape(b, s, 3 * rw + lora_cols)

    bias_row = jnp.zeros((1, LANES), F32).at[0, :fox_heads].set(f_bias.astype(F32))
    c2 = _gate_cumsum(ff.reshape(b, s, LANES), bias_row, 0, 512)
    hf = _fox_attention(qkvz, c2, fox_heads, min(512, s), 0, 2)

    pad_rows = lambda w, off: jnp.zeros((lora_cols, rw), F32).at[off:off + w.shape[0]].set(w.astype(F32)).astype(BF16)
    rows = {}
    for name, val in (("w0", w0), ("a0", a0), ("k_k", k_k), ("k_a", k_a), ("r_k", r_k),
                      ("gn_gain", gn_gain), ("gn_bias", gn_bias)):
        rows[name] = val.astype(F32).reshape(1, rw)
    hr = _rwkv_mix(rkvl, qkvz, rows, pad_rows(w_up, 0), pad_rows(a_up, dl), n_pairs, lora_cols,
                   3 * fw + fw, 256, 2)

    out = _out_proj_layernorm(hf.reshape(b * s, fw), hr.reshape(b * s, rw), w_out.astype(BF16),
                              x2, ln_gain, ln_bias, alpha, 512, 512)
    return out.reshape(b, s, d)
```

```python
import functools
import math

import jax
import jax.numpy as jnp
from jax import lax
from jax.experimental import pallas as pl
from jax.experimental.pallas import tpu as pltpu

F32 = jnp.float32
BF16 = jnp.bfloat16

LANES = 128
BF16_SUBLANES = 16
FOX_HEAD_DIM = 128
RWKV_HEAD_DIM = 64
RWKV_CHUNK = 64
LOG2_E = 1.4426950408889634
EXP_NEG_HALF = 0.6065306597126334
LN_EPS = 1e-5
GN_EPS = 64e-5
VMEM_LIMIT_BYTES = 56 * 1024 * 1024


def _bdot(a, b, dims=(((1,), (0,)), ((), ()))):
    return lax.dot_general(a.astype(BF16), b.astype(BF16), dims, preferred_element_type=F32)


def _sigmoid(x):
    return 0.5 * jnp.tanh(0.5 * x) + 0.5


def _split3_dot(m_bf16, x):
    hi = x.astype(BF16)
    r1 = x - hi.astype(F32)
    mid = r1.astype(BF16)
    lo = (r1 - mid.astype(F32)).astype(BF16)
    dot = functools.partial(jnp.dot, preferred_element_type=F32)
    return dot(m_bf16, hi) + dot(m_bf16, mid) + dot(m_bf16, lo)


def _split2_dot(m_bf16, x):
    hi = x.astype(BF16)
    lo = (x - hi.astype(F32)).astype(BF16)
    dot = functools.partial(jnp.dot, preferred_element_type=F32)
    return dot(m_bf16, hi) + dot(m_bf16, lo)


def _pick_tile(n, preferred):
    t = min(preferred, n) // LANES * LANES
    while n % t:
        t -= LANES
    return t


_NT = (((1,), (1,)), ((), ()))


def _proj_fox_kernel(a_ref, wt_ref, o_ref, *, n_q_blocks, n_qkv_blocks, q_scale):
    j = pl.program_id(1)
    acc = lax.dot_general(a_ref[...], wt_ref[...], _NT, preferred_element_type=F32)
    factor = jnp.where(j >= n_qkv_blocks, _sigmoid(acc), jnp.where(j < n_q_blocks, q_scale, 1.0))
    o_ref[...] = (acc * factor).astype(o_ref.dtype)


def _proj_rwkv_kernel(a_ref, wt_ref, et_ref, mu_ref, o_ref, e_ref, last_ref, *, blocks_per_seq):
    i, j = pl.program_id(0), pl.program_id(1)

    @pl.when(j == 0)
    def _():
        e_ref[...] = lax.dot_general(a_ref[...], et_ref[...], _NT, preferred_element_type=F32)

    p = lax.dot_general(a_ref[...], wt_ref[...], _NT, preferred_element_type=F32)
    tm = p.shape[0]
    above = jnp.where(i % blocks_per_seq == 0, 0.0, last_ref[j])
    first_row = lax.broadcasted_iota(jnp.int32, (tm, 1), 0) == 0
    prev = jnp.where(first_row, above, pltpu.roll(p, 1, axis=0))
    last_ref[j] = p[tm - 1:tm, :]
    o_ref[...] = p + (prev - p) * mu_ref[...]


def _w_row_map(segments, tn):
    def w_row(j):
        row, first = None, 0
        for nb, row0 in segments:
            here = row0 + (j - first) * tn
            row = here if row is None else jnp.where(j >= first, here, row)
            first += nb
        return pl.multiple_of(row, BF16_SUBLANES)
    return w_row


def _in_proj_fox(a, w_t, tm, tn, n_q_blocks, n_qkv_blocks, z_row0, n_z_blocks, q_scale):
    m, k = a.shape
    tm = _pick_tile(m, tm)
    w_row = _w_row_map([(n_qkv_blocks, 0), (n_z_blocks, z_row0)], tn)
    n_blocks = n_qkv_blocks + n_z_blocks
    return pl.pallas_call(
        functools.partial(_proj_fox_kernel, n_q_blocks=n_q_blocks, n_qkv_blocks=n_qkv_blocks, q_scale=q_scale),
        grid=(m // tm, n_blocks),
        in_specs=[pl.BlockSpec((tm, k), lambda i, j: (i, 0)),
                  pl.BlockSpec((pl.Element(tn), pl.Element(k)), lambda i, j: (w_row(j), 0))],
        out_specs=pl.BlockSpec((tm, tn), lambda i, j: (i, j)),
        out_shape=jax.ShapeDtypeStruct((m, n_blocks * tn), BF16),
        name="in_proj_fox",
        compiler_params=pltpu.CompilerParams(
            dimension_semantics=("parallel", "arbitrary"), vmem_limit_bytes=VMEM_LIMIT_BYTES),
    )(a, w_t)


def _in_proj_rwkv(a, w_t, mu_row, tm, tn, row0, n_blocks, gate_row0, seq_len):
    m, k = a.shape
    tm = _pick_tile(math.gcd(m, seq_len), tm)
    w_row = _w_row_map([(n_blocks, row0)], tn)
    return pl.pallas_call(
        functools.partial(_proj_rwkv_kernel, blocks_per_seq=seq_len // tm),
        grid=(m // tm, n_blocks),
        in_specs=[pl.BlockSpec((tm, k), lambda i, j: (i, 0)),
                  pl.BlockSpec((pl.Element(tn), pl.Element(k)), lambda i, j: (w_row(j), 0)),
                  pl.BlockSpec((pl.Element(LANES), pl.Element(k)), lambda i, j: (gate_row0, 0)),
                  pl.BlockSpec((1, tn), lambda i, j: (0, j))],
        out_specs=[pl.BlockSpec((tm, tn), lambda i, j: (i, j)),
                   pl.BlockSpec((tm, LANES), lambda i, j: (i, 0))],
        out_shape=[jax.ShapeDtypeStruct((m, n_blocks * tn), F32), jax.ShapeDtypeStruct((m, LANES), F32)],
        name="in_proj_rwkv",
        scratch_shapes=[pltpu.VMEM((n_blocks, 1, tn), F32)],
        compiler_params=pltpu.CompilerParams(
            dimension_semantics=("arbitrary", "arbitrary"), vmem_limit_bytes=VMEM_LIMIT_BYTES),
    )(a, w_t, w_t, mu_row)


def _gate_kernel(x_ref, b_ref, o_ref, carry_ref):
    @pl.when(pl.program_id(1) == 0)
    def _():
        carry_ref[...] = jnp.zeros_like(carry_ref)

    t = x_ref.shape[1]
    x = x_ref[0] + b_ref[...]
    logf = jnp.minimum(x, 0.0) - jnp.log1p(jnp.exp(-jnp.abs(x)))
    tri = (lax.broadcasted_iota(jnp.int32, (t, t), 1)
           <= lax.broadcasted_iota(jnp.int32, (t, t), 0)).astype(BF16)
    c = _split3_dot(tri, logf) + carry_ref[0:1, :]
    o_ref[0] = c * LOG2_E
    carry_ref[0:1, :] = c[t - 1:t, :]


def _gate_cumsum(lora3, bias_row, col_block, t):
    b, s, _ = lora3.shape
    t = min(t, s)
    return pl.pallas_call(
        _gate_kernel,
        grid=(b, s // t),
        in_specs=[pl.BlockSpec((1, t, LANES), lambda bi, i: (bi, i, col_block)),
                  pl.BlockSpec((1, LANES), lambda bi, i: (0, 0))],
        out_specs=pl.BlockSpec((1, t, LANES), lambda bi, i: (bi, i, 0)),
        out_shape=jax.ShapeDtypeStruct((b, s, LANES), F32),
        name="fox_gate_cumsum",
        scratch_shapes=[pltpu.VMEM((8, LANES), F32)],
        compiler_params=pltpu.CompilerParams(dimension_semantics=("parallel", "arbitrary")),
    )(lora3, bias_row)


def _fox_kernel(q_ref, k_ref, v_ref, c_ref, z_ref, o_ref, kaug_ref, vt_ref, sa_ref, sb_ref, pa_ref, pb_ref,
                m_ref, al_ref, acc_ref, *, tq, c_lane0, group_cols):
    i = pl.program_id(2)
    s_len = k_ref.shape[1]
    dh = FOX_HEAD_DIM
    n_heads = k_ref.shape[2] // dh
    gw = group_cols
    n_groups = tq // gw
    units = [(hh, g) for hh in range(n_heads) for g in range(n_groups)]
    cols = [slice(u * gw, (u + 1) * gw) for u in range(len(units))]

    @pl.when(i == 0)
    def _():
        lane_r = lax.broadcasted_iota(jnp.int32, (LANES, LANES), 0)
        lane_c = lax.broadcasted_iota(jnp.int32, (LANES, LANES), 1)
        dot = functools.partial(jnp.dot, preferred_element_type=F32)
        for hh in range(n_heads):
            c_lane = c_lane0 + pl.program_id(1) * n_heads + hh
            pick = [jnp.where((lane_r == c_lane) & (lane_c == term), 1.0, 0.0).astype(BF16) for term in range(3)]
            hs = slice(hh * dh, (hh + 1) * dh)
            for jb in range(s_len // tq):
                sl = slice(jb * tq, (jb + 1) * tq)
                c = c_ref[0, sl, :]
                hi = c.astype(BF16)
                r1 = c - hi.astype(F32)
                mid = r1.astype(BF16)
                lo = (r1 - mid.astype(F32)).astype(BF16)
                c_terms = dot(hi, pick[0]) + dot(mid, pick[1]) + dot(lo, pick[2])
                kaug_ref[hh, sl, :dh] = k_ref[0, sl, hs]
                kaug_ref[hh, sl, dh:] = (-c_terms).astype(BF16)
                vt_ref[hh, jb, :dh, :] = v_ref[0, sl, hs].astype(F32).T.astype(BF16)
                vt_ref[hh, jb, dh:, :] = jnp.ones((vt_ref.shape[2] - dh, tq), BF16)

    ones3 = jnp.where(lax.broadcasted_iota(jnp.int32, (dh, tq), 0) < 3, 1.0, 0.0)
    q_t = [jnp.concatenate([q_ref[0, :, hh * dh:(hh + 1) * dh].astype(F32).T, ones3], axis=0).astype(BF16)
           for hh in range(n_heads)]
    q_u = [q_t[hh][:, g * gw:(g + 1) * gw] for (hh, g) in units]

    def scores(j, s_out):
        rows = pl.ds(pl.multiple_of(j * tq, tq), tq)
        for u, (hh, _) in enumerate(units):
            s_out[:, cols[u]] = jnp.dot(kaug_ref[hh, rows, :], q_u[u], preferred_element_type=F32)

    def accumulate(p, j):
        for u, (hh, _) in enumerate(units):
            acc_ref[:, cols[u]] = (al_ref[:, cols[u]] * acc_ref[:, cols[u]]
                                   + jnp.dot(vt_ref[hh, j], p[u], preferred_element_type=F32))

    def softmax_step(s):
        p = []
        for u in range(len(units)):
            m = m_ref[:, cols[u]]
            m_new = jnp.maximum(m, jnp.max(s[u], axis=0, keepdims=True))
            p.append(jnp.exp2(s[u] - m_new).astype(BF16))
            al_ref[:, cols[u]] = jnp.exp2(m - m_new)
            m_ref[:, cols[u]] = m_new
        return p

    def step(j, s_in, s_out, p_in, p_out):
        scores(j + 1, s_out)
        accumulate([p_in[:, cs] for cs in cols], jnp.maximum(j - 1, 0))
        p = softmax_step([s_in[:, cs] for cs in cols])
        for u, cs in enumerate(cols):
            p_out[:, cs] = p[u]

    scores(0, sa_ref)
    pa_ref[...] = jnp.zeros_like(pa_ref)
    m_ref[...] = jnp.full(m_ref.shape, -jnp.inf, F32)
    al_ref[...] = jnp.ones_like(al_ref)
    acc_ref[...] = jnp.zeros_like(acc_ref)

    def pair(t, carry):
        step(2 * t, sa_ref, sb_ref, pa_ref, pb_ref)
        step(2 * t + 1, sb_ref, sa_ref, pb_ref, pa_ref)
        return carry

    lax.fori_loop(0, i // 2, pair, 0)

    def finish(s_last, p_last):
        accumulate([p_last[:, cs] for cs in cols], jnp.maximum(i - 1, 0))
        key = lax.broadcasted_iota(jnp.int32, (tq, gw), 0)
        qry = lax.broadcasted_iota(jnp.int32, (tq, gw), 1)
        p = softmax_step([jnp.where(key <= qry + g * gw, s_last[:, cols[u]], -jnp.inf)
                          for u, (_, g) in enumerate(units)])
        accumulate(p, i)
        for u, (hh, g) in enumerate(units):
            acc_u = acc_ref[:, cols[u]]
            gate = z_ref[0, g * gw:(g + 1) * gw, hh * dh:(hh + 1) * dh].astype(F32)
            out = (acc_u[:dh, :] / acc_u[dh:dh + 1, :]).T
            o_ref[0, g * gw:(g + 1) * gw, hh * dh:(hh + 1) * dh] = (out * gate).astype(o_ref.dtype)

    @pl.when(i % 2 == 1)
    def _():
        step(i - 1, sa_ref, sb_ref, pa_ref, pb_ref)
        finish(sb_ref, pb_ref)

    @pl.when(i % 2 == 0)
    def _():
        finish(sa_ref, pa_ref)


def _fox_attention(qkvz, c2, n_heads, tq, c_lane0, heads_per_step):
    b, s, _ = qkvz.shape
    nq = s // tq
    nh = heads_per_step
    hb = n_heads // nh
    assert hb * nh == n_heads
    w = nh * FOX_HEAD_DIM
    acc_rows = FOX_HEAD_DIM + BF16_SUBLANES
    return pl.pallas_call(
        functools.partial(_fox_kernel, tq=tq, c_lane0=c_lane0, group_cols=min(256, tq)),
        grid=(b, hb, nq),
        in_specs=[pl.BlockSpec((1, tq, w), lambda bi, hi, i: (bi, i, hi)),
                  pl.BlockSpec((1, s, w), lambda bi, hi, i: (bi, 0, hb + hi)),
                  pl.BlockSpec((1, s, w), lambda bi, hi, i: (bi, 0, 2 * hb + hi)),
                  pl.BlockSpec((1, s, LANES), lambda bi, hi, i: (bi, 0, 0)),
                  pl.BlockSpec((1, tq, w), lambda bi, hi, i: (bi, i, 3 * hb + hi))],
        out_specs=pl.BlockSpec((1, tq, w), lambda bi, hi, i: (bi, i, hi)),
        out_shape=jax.ShapeDtypeStruct((b, s, n_heads * FOX_HEAD_DIM), BF16),
        name="fox_attention",
        scratch_shapes=[pltpu.VMEM((nh, s, 2 * FOX_HEAD_DIM), BF16),
                        pltpu.VMEM((nh, nq, acc_rows, tq), BF16),
                        pltpu.VMEM((tq, nh * tq), F32), pltpu.VMEM((tq, nh * tq), F32),
                        pltpu.VMEM((tq, nh * tq), BF16), pltpu.VMEM((tq, nh * tq), BF16),
                        pltpu.VMEM((1, nh * tq), F32), pltpu.VMEM((1, nh * tq), F32),
                        pltpu.VMEM((acc_rows, nh * tq), F32)],
        compiler_params=pltpu.CompilerParams(
            dimension_semantics=("parallel", "parallel", "arbitrary"), vmem_limit_bytes=VMEM_LIMIT_BYTES),
    )(qkvz, qkvz, qkvz, c2, qkvz)


def _stack_heads(x, head0):
    zero = jnp.zeros_like(x)
    return jnp.concatenate([jnp.where(head0, x, zero), jnp.where(head0, zero, x)], axis=0)


def _rwkv_chunks(units, masks):
    head0, strict, diag, blockdiag, eye = masks
    incl = strict | diag
    c = units[0]["r"].shape[0]
    n_units = range(len(units))
    zero = jnp.zeros((c, LANES), F32)
    nt = (((1,), (1,)), ((), ()))
    tn = (((0,), (0,)), ((), ()))

    rt, at, bh, kh, gc, sv, lhs1, rhs1 = [], [], [], [], [], [], [], []
    for u in units:
        lg = u["lg"]
        lgc = lg[c - 1:c, :]
        e_neg = jnp.exp(-lg)
        e_end = jnp.exp(lgc - lg)
        rt.append(u["r"] * jnp.exp(lg))
        at.append(u["av"] * jnp.exp(lg - u["logw"]))
        bh.append(u["bv"] * e_end)
        kh.append(u["k2"] * e_end)
        gc.append(jnp.exp(lgc))
        sv.append(_stack_heads(u["v"], head0))
        lhs1.append(jnp.concatenate([at[-1], rt[-1]], axis=0))
        rhs1.append(jnp.concatenate([_stack_heads(u["bv"] * e_neg, head0),
                                     _stack_heads(u["k2"] * e_neg, head0)], axis=0))

    big = [_bdot(lhs1[i], rhs1[i], nt) for i in n_units]
    l_mat = [jnp.where(strict, big[i][:c, :LANES], zero) for i in n_units]
    a_ak = [jnp.where(strict, big[i][:c, LANES:], zero) for i in n_units]
    a_r = [jnp.concatenate([jnp.where(incl, big[i][c:, :LANES], zero),
                            jnp.where(incl, big[i][c:, LANES:], zero)], axis=1) for i in n_units]

    aakv = [_bdot(a_ak[i], sv[i]) for i in n_units]
    n_steps = max(1, (c - 1).bit_length())
    t_inv = [jnp.where(diag, 1.0, 0.0) + l_mat[i] for i in n_units]
    cur = [_bdot(l_mat[i], _stack_heads(l_mat[i], head0)) for i in n_units] if n_steps > 1 else None
    for step in range(1, n_steps):
        if step + 1 < n_steps:
            both = [_bdot(jnp.concatenate([cur[i], t_inv[i]], axis=0), _stack_heads(cur[i], head0)) for i in n_units]
            t_inv = [t_inv[i] + both[i][c:] for i in n_units]
            cur = [both[i][:c] for i in n_units]
        else:
            t_inv = [t_inv[i] + _bdot(t_inv[i], _stack_heads(cur[i], head0)) for i in n_units]
    x = [_bdot(t_inv[i], jnp.concatenate([_stack_heads(at[i], head0), _stack_heads(aakv[i], head0)], axis=1))
         for i in n_units]

    rhs2 = [jnp.concatenate(
        [jnp.concatenate([_stack_heads(x[i][:, :LANES], head0), _stack_heads(x[i][:, LANES:], head0)], axis=1),
         jnp.concatenate([jnp.zeros((2 * c, LANES), F32), sv[i]], axis=1)], axis=0) for i in n_units]
    qy = [_bdot(a_r[i], rhs2[i]) for i in n_units]
    gh = [_bdot(jnp.concatenate([bh[i], kh[i]], axis=0),
                jnp.concatenate([x[i], jnp.concatenate([zero, units[i]["v"]], axis=1)], axis=0), tn)
          for i in n_units]
    out = []
    for i in n_units:
        out.append((rt[i] + qy[i][:, :LANES], qy[i][:, LANES:],
                    jnp.where(blockdiag, gh[i][:, :LANES], 0.0) + eye * gc[i],
                    jnp.where(blockdiag, gh[i][:, LANES:], 0.0)))
    return out


def _rwkv_kernel(r_ref, k_ref, v_ref, lo_ref, z_ref, wup_ref, aup_ref, w0_ref, a0_ref, kk_ref, ka_ref, rk_ref,
                 gg_ref, gb_ref, o_ref, sbd_ref):
    nb, t, w = r_ref.shape
    c = RWKV_CHUNK
    n = RWKV_HEAD_DIM
    n_pairs = w // LANES
    n_chunks = t // c

    @pl.when(pl.program_id(1) == 0)
    def _():
        sbd_ref[...] = jnp.zeros_like(sbd_ref)

    wr = lax.broadcasted_iota(jnp.int32, (w, w), 0)
    wc = lax.broadcasted_iota(jnp.int32, (w, w), 1)
    head_ones = ((wr // n) == (wc // n)).astype(BF16)
    tr = lax.broadcasted_iota(jnp.int32, (t, t), 0)
    tc = lax.broadcasted_iota(jnp.int32, (t, t), 1)
    chunk_tri = ((tc <= tr) & ((tc // c) == (tr // c))).astype(BF16)

    def head_sum(x):
        return _bdot(x, head_ones)

    prepped = []
    for bi in range(nb):
        r, k, v, lora = r_ref[bi], k_ref[bi], v_ref[bi], lo_ref[bi]
        dec = _bdot(jnp.tanh(lora), wup_ref[...])
        logw = -EXP_NEG_HALF * _sigmoid(w0_ref[...] + dec)
        a = _sigmoid(a0_ref[...] + _bdot(lora, aup_ref[...]))
        kk = k * kk_ref[...]
        kk = kk * lax.rsqrt(jnp.maximum(head_sum(kk * kk), 1e-24))
        k2 = k * (1.0 + (a - 1.0) * ka_ref[...])
        prepped.append(dict(r=r, k2=k2, v=v, av=-kk, bv=kk * a, logw=logw, lg=_split2_dot(chunk_tri, logw)))

    lane = lax.broadcasted_iota(jnp.int32, (c, LANES), 1)
    row = lax.broadcasted_iota(jnp.int32, (c, LANES), 0)
    head0 = lane < n
    col = jnp.where(head0, lane, lane - n)
    lane_r = lax.broadcasted_iota(jnp.int32, (LANES, LANES), 0)
    lane_c = lax.broadcasted_iota(jnp.int32, (LANES, LANES), 1)
    masks = (head0, col < row, col == row, (lane_r < n) == (lane_c < n), (lane_r == lane_c).astype(F32))

    seqs = [(bi, pi) for bi in range(nb) for pi in range(n_pairs)]
    units = [{name: arr[ci * c:(ci + 1) * c, pi * LANES:(pi + 1) * LANES] for name, arr in prepped[bi].items()}
             for ci in range(n_chunks) for (bi, pi) in seqs]
    affine = _rwkv_chunks(units, masks)

    sbd = [sbd_ref[si] for si in range(len(seqs))]
    ys = {}
    for ci in range(n_chunks):
        qs = [affine[ci * len(seqs) + si] for si in range(len(seqs))]
        both = [_bdot(jnp.concatenate([qs[si][0], qs[si][2]], axis=0), sbd[si]) for si in range(len(seqs))]
        sbd = [both[si][c:] + qs[si][3] for si in range(len(seqs))]
        for si, seq in enumerate(seqs):
            ys[seq + (ci,)] = both[si][:c] + qs[si][1]
    for si in range(len(seqs)):
        sbd_ref[si] = sbd[si]

    for bi in range(nb):
        y = jnp.concatenate([jnp.concatenate([ys[(bi, pi, ci)] for pi in range(n_pairs)], axis=1)
                             for ci in range(n_chunks)], axis=0)
        p = prepped[bi]
        mu = head_sum(y) * (1.0 / n)
        d = y - mu
        var = head_sum(d * d) * (1.0 / n)
        y = d * lax.rsqrt(var + GN_EPS) * gg_ref[...] + gb_ref[...]
        y = y + head_sum(p["r"] * p["k2"] * rk_ref[...]) * p["v"]
        o_ref[bi] = (y * z_ref[bi].astype(F32)).astype(o_ref.dtype)


def _rwkv_mix(rkvl, qkvz, rows, wup_pad, aup_pad, n_pairs, lora_cols, z_col0, t, pairs_per_step):
    b, s, _ = rkvl.shape
    t = min(t, s)
    w = pairs_per_step * LANES
    p = n_pairs // pairs_per_step
    assert p * pairs_per_step == n_pairs and z_col0 % w == 0
    lora_block = 3 * n_pairs * LANES // lora_cols
    assert lora_block * lora_cols == 3 * n_pairs * LANES
    tok = lambda off: pl.BlockSpec((b, t, w), lambda j, i: (0, i, off + j))
    prow = lambda off: pl.BlockSpec((1, w), lambda j, i: (0, off + j))
    in_specs = [
        tok(0), tok(p), tok(2 * p),
        pl.BlockSpec((b, t, lora_cols), lambda j, i: (0, i, lora_block)),
        tok(z_col0 // w),
        pl.BlockSpec((lora_cols, w), lambda j, i: (0, j)),
        pl.BlockSpec((lora_cols, w), lambda j, i: (0, j)),
    ] + [prow(0)] * 7
    return pl.pallas_call(
        _rwkv_kernel,
        grid=(p, s // t),
        in_specs=in_specs,
        out_specs=pl.BlockSpec((b, t, w), lambda j, i: (0, i, j)),
        out_shape=jax.ShapeDtypeStruct((b, s, n_pairs * LANES), BF16),
        name="rwkv7_mix",
        scratch_shapes=[pltpu.VMEM((b * pairs_per_step, LANES, LANES), F32)],
        compiler_params=pltpu.CompilerParams(
            dimension_semantics=("parallel", "arbitrary"), vmem_limit_bytes=VMEM_LIMIT_BYTES),
    )(rkvl, rkvl, rkvl, rkvl, qkvz, wup_pad, aup_pad,
      rows["w0"], rows["a0"], rows["k_k"], rows["k_a"], rows["r_k"], rows["gn_gain"], rows["gn_bias"])


def _out_kernel(hf_ref, hr_ref, wt_ref, wb_ref, x_ref, g_ref, b_ref, o_ref, acc_ref, sum_ref, *, alpha, n_tiles):
    j = pl.program_id(1)
    out = (jnp.dot(hf_ref[...], wt_ref[...], preferred_element_type=F32)
           + jnp.dot(hr_ref[...], wb_ref[...], preferred_element_type=F32))
    pre = alpha * x_ref[...] + out
    acc_ref[j] = pre
    part = jnp.sum(pre, axis=-1, keepdims=True)
    sum_ref[...] = jnp.where(j == 0, part, sum_ref[...] + part)

    @pl.when(j == n_tiles - 1)
    def _():
        tn = acc_ref.shape[2]
        d = n_tiles * tn
        mean = sum_ref[...] * (1.0 / d)
        sq = jnp.sum(jnp.square(acc_ref[0] - mean), axis=-1, keepdims=True)
        for jj in range(1, n_tiles):
            sq = sq + jnp.sum(jnp.square(acc_ref[jj] - mean), axis=-1, keepdims=True)
        inv = lax.rsqrt(sq * (1.0 / d) + LN_EPS)
        for jj in range(n_tiles):
            sl = slice(jj * tn, (jj + 1) * tn)
            o_ref[:, sl] = ((acc_ref[jj] - mean) * inv * g_ref[:, sl] + b_ref[:, sl]).astype(o_ref.dtype)


def _out_proj_layernorm(hf, hr, w_out_bf16, x2, ln_gain, ln_bias, alpha, tm, tn):
    m, d = x2.shape
    kf, kr = hf.shape[1], hr.shape[1]
    assert kf == kr and kf + kr == w_out_bf16.shape[0]
    tm, tn = min(tm, m), min(tn, d)
    n_tiles = d // tn
    return pl.pallas_call(
        functools.partial(_out_kernel, alpha=alpha, n_tiles=n_tiles),
        grid=(m // tm, n_tiles),
        in_specs=[pl.BlockSpec((tm, kf), lambda i, j: (i, 0)),
                  pl.BlockSpec((tm, kr), lambda i, j: (i, 0)),
                  pl.BlockSpec((kf, tn), lambda i, j: (0, j)),
                  pl.BlockSpec((kr, tn), lambda i, j: (1, j)),
                  pl.BlockSpec((tm, tn), lambda i, j: (i, j)),
                  pl.BlockSpec((1, d), lambda i, j: (0, 0)),
                  pl.BlockSpec((1, d), lambda i, j: (0, 0))],
        out_specs=pl.BlockSpec((tm, d), lambda i, j: (i, 0)),
        out_shape=jax.ShapeDtypeStruct((m, d), x2.dtype),
        name="out_proj_layernorm",
        scratch_shapes=[pltpu.VMEM((n_tiles, tm, tn), F32), pltpu.VMEM((tm, 1), F32)],
        compiler_params=pltpu.CompilerParams(
            dimension_semantics=("parallel", "arbitrary"), vmem_limit_bytes=VMEM_LIMIT_BYTES),
    )(hf, hr, w_out_bf16, w_out_bf16, x2, ln_gain.reshape(1, d).astype(F32), ln_bias.reshape(1, d).astype(F32))


def kernel(x, w_in, f_bias, mu_shift, w0, w_up, a0, a_up, k_k, k_a, r_k, gn_gain, gn_bias, w_out, ln_gain, ln_bias):
    b, s, d = x.shape
    fox_heads = f_bias.shape[0]
    fw = fox_heads * FOX_HEAD_DIM
    rw = w0.shape[0]
    n_pairs = rw // LANES
    assert r_k.shape[1] == RWKV_HEAD_DIM and rw % LANES == 0
    dl, al = w_up.shape[0], a_up.shape[0]
    mix = fw + rw
    assert w_in.shape[1] == 3 * fw + fox_heads + 3 * rw + dl + al + mix
    depth = 1
    alpha = (2 * depth) ** 0.25

    o_ff = 3 * fw
    o_r = o_ff + fox_heads
    o_wd = o_r + 3 * rw
    o_ad = o_wd + dl
    o_z = o_ad + al

    lora_cols = -(-(dl + al) // 256) * 256
    assert (3 * rw) % lora_cols == 0 and fox_heads <= LANES
    assert o_r + 3 * rw + lora_cols <= w_in.shape[1] and o_ff + LANES <= w_in.shape[1]
    assert all(off % BF16_SUBLANES == 0 for off in (o_ff, o_r, o_z)), "weight row windows must be tile aligned"

    x2 = x.reshape(b * s, d)
    w_t = w_in.T.astype(BF16)
    tn_bf = _pick_tile(math.gcd(fw, mix), 1024)
    tn_f32 = _pick_tile(3 * rw + lora_cols, 1280)
    mu_row = jnp.concatenate([mu_shift.astype(F32), jnp.zeros((lora_cols - dl - al,), F32)]).reshape(1, -1)

    xb = x2.astype(BF16)
    qkvz = _in_proj_fox(xb, w_t, 1024, tn_bf, fw // tn_bf, 3 * fw // tn_bf, o_z, mix // tn_bf,
                        LOG2_E * FOX_HEAD_DIM ** -0.5).reshape(b, s, 3 * fw + mix)
    rkvl, ff = _in_proj_rwkv(xb, w_t, mu_row, 512, tn_f32, o_r, (3 * rw + lora_cols) // tn_f32, o_ff, s)
    rkvl = rkvl.reshape(b, s, 3 * rw + lora_cols)

    bias_row = jnp.zeros((1, LANES), F32).at[0, :fox_heads].set(f_bias.astype(F32))
    c2 = _gate_cumsum(ff.reshape(b, s, LANES), bias_row, 0, 512)
    hf = _fox_attention(qkvz, c2, fox_heads, min(512, s), 0, 2)

    pad_rows = lambda w, off: jnp.zeros((lora_cols, rw), F32).at[off:off + w.shape[0]].set(w.astype(F32)).astype(BF16)
    rows = {}
    for name, val in (("w0", w0), ("a0", a0), ("k_k", k_k), ("k_a", k_a), ("r_k", r_k),
                      ("gn_gain", gn_gain), ("gn_bias", gn_bias)):
        rows[name] = val.astype(F32).reshape(1, rw)
    hr = _rwkv_mix(rkvl, qkvz, rows, pad_rows(w_up, 0), pad_rows(a_up, dl), n_pairs, lora_cols,
                   3 * fw + fw, 512, 2)

    out = _out_proj_layernorm(hf.reshape(b * s, fw), hr.reshape(b * s, rw), w_out.astype(BF16),
                              x2, ln_gain, ln_bias, alpha, 512, 512)
    return out.reshape(b, s, d)
```

```python
import functools
import math

import jax
import jax.numpy as jnp
from jax import lax
from jax.experimental import pallas as pl
from jax.experimental.pallas import tpu as pltpu

F32 = jnp.float32
BF16 = jnp.bfloat16

LANES = 128
BF16_SUBLANES = 16
FOX_HEAD_DIM = 128
RWKV_HEAD_DIM = 64
RWKV_CHUNK = 64
LOG2_E = 1.4426950408889634
EXP_NEG_HALF = 0.6065306597126334
LN_EPS = 1e-5
GN_EPS = 64e-5
VMEM_LIMIT_BYTES = 56 * 1024 * 1024


def _bdot(a, b, dims=(((1,), (0,)), ((), ()))):
    return lax.dot_general(a.astype(BF16), b.astype(BF16), dims, preferred_element_type=F32)


def _sigmoid(x):
    return 0.5 * jnp.tanh(0.5 * x) + 0.5


def _split3_dot(m_bf16, x):
    hi = x.astype(BF16)
    r1 = x - hi.astype(F32)
    mid = r1.astype(BF16)
    lo = (r1 - mid.astype(F32)).astype(BF16)
    dot = functools.partial(jnp.dot, preferred_element_type=F32)
    return dot(m_bf16, hi) + dot(m_bf16, mid) + dot(m_bf16, lo)


def _split2_dot(m_bf16, x):
    hi = x.astype(BF16)
    lo = (x - hi.astype(F32)).astype(BF16)
    dot = functools.partial(jnp.dot, preferred_element_type=F32)
    return dot(m_bf16, hi) + dot(m_bf16, lo)


def _pick_tile(n, preferred):
    t = min(preferred, n) // LANES * LANES
    while n % t:
        t -= LANES
    return t


_NT = (((1,), (1,)), ((), ()))


def _proj_fox_kernel(a_ref, wt_ref, o_ref, *, n_q_blocks, n_qkv_blocks, q_scale):
    j = pl.program_id(1)
    acc = lax.dot_general(a_ref[...], wt_ref[...], _NT, preferred_element_type=F32)
    factor = jnp.where(j >= n_qkv_blocks, _sigmoid(acc), jnp.where(j < n_q_blocks, q_scale, 1.0))
    o_ref[...] = (acc * factor).astype(o_ref.dtype)


def _proj_rwkv_kernel(a_ref, wt_ref, et_ref, mu_ref, o_ref, e_ref, last_ref, *, blocks_per_seq):
    i, j = pl.program_id(0), pl.program_id(1)

    @pl.when(j == 0)
    def _():
        e_ref[...] = lax.dot_general(a_ref[...], et_ref[...], _NT, preferred_element_type=F32)

    p = lax.dot_general(a_ref[...], wt_ref[...], _NT, preferred_element_type=F32)
    tm = p.shape[0]
    above = jnp.where(i % blocks_per_seq == 0, 0.0, last_ref[j])
    first_row = lax.broadcasted_iota(jnp.int32, (tm, 1), 0) == 0
    prev = jnp.where(first_row, above, pltpu.roll(p, 1, axis=0))
    last_ref[j] = p[tm - 1:tm, :]
    o_ref[...] = p + (prev - p) * mu_ref[...]


def _w_row_map(segments, tn):
    def w_row(j):
        row, first = None, 0
        for nb, row0 in segments:
            here = row0 + (j - first) * tn
            row = here if row is None else jnp.where(j >= first, here, row)
            first += nb
        return pl.multiple_of(row, BF16_SUBLANES)
    return w_row


def _in_proj_fox(a, w_t, tm, tn, n_q_blocks, n_qkv_blocks, z_row0, n_z_blocks, q_scale):
    m, k = a.shape
    tm = _pick_tile(m, tm)
    w_row = _w_row_map([(n_qkv_blocks, 0), (n_z_blocks, z_row0)], tn)
    n_blocks = n_qkv_blocks + n_z_blocks
    return pl.pallas_call(
        functools.partial(_proj_fox_kernel, n_q_blocks=n_q_blocks, n_qkv_blocks=n_qkv_blocks, q_scale=q_scale),
        grid=(m // tm, n_blocks),
        in_specs=[pl.BlockSpec((tm, k), lambda i, j: (i, 0)),
                  pl.BlockSpec((pl.Element(tn), pl.Element(k)), lambda i, j: (w_row(j), 0))],
        out_specs=pl.BlockSpec((tm, tn), lambda i, j: (i, j)),
        out_shape=jax.ShapeDtypeStruct((m, n_blocks * tn), BF16),
        name="in_proj_fox",
        compiler_params=pltpu.CompilerParams(
            dimension_semantics=("parallel", "arbitrary"), vmem_limit_bytes=VMEM_LIMIT_BYTES),
    )(a, w_t)


def _in_proj_rwkv(a, w_t, mu_row, tm, tn, row0, n_blocks, gate_row0, seq_len):
    m, k = a.shape
    tm = _pick_tile(math.gcd(m, seq_len), tm)
    w_row = _w_row_map([(n_blocks, row0)], tn)
    return pl.pallas_call(
        functools.partial(_proj_rwkv_kernel, blocks_per_seq=seq_len // tm),
        grid=(m // tm, n_blocks),
        in_specs=[pl.BlockSpec((tm, k), lambda i, j: (i, 0)),
                  pl.BlockSpec((pl.Element(tn), pl.Element(k)), lambda i, j: (w_row(j), 0)),
                  pl.BlockSpec((pl.Element(LANES), pl.Element(k)), lambda i, j: (gate_row0, 0)),
                  pl.BlockSpec((1, tn), lambda i, j: (0, j))],
        out_specs=[pl.BlockSpec((tm, tn), lambda i, j: (i, j)),
                   pl.BlockSpec((tm, LANES), lambda i, j: (i, 0))],
        out_shape=[jax.ShapeDtypeStruct((m, n_blocks * tn), F32), jax.ShapeDtypeStruct((m, LANES), F32)],
        name="in_proj_rwkv",
        scratch_shapes=[pltpu.VMEM((n_blocks, 1, tn), F32)],
        compiler_params=pltpu.CompilerParams(
            dimension_semantics=("arbitrary", "arbitrary"), vmem_limit_bytes=VMEM_LIMIT_BYTES),
    )(a, w_t, w_t, mu_row)


def _gate_kernel(x_ref, b_ref, o_ref, carry_ref):
    @pl.when(pl.program_id(1) == 0)
    def _():
        carry_ref[...] = jnp.zeros_like(carry_ref)

    t = x_ref.shape[1]
    x = x_ref[0] + b_ref[...]
    logf = jnp.minimum(x, 0.0) - jnp.log1p(jnp.exp(-jnp.abs(x)))
    tri = (lax.broadcasted_iota(jnp.int32, (t, t), 1)
           <= lax.broadcasted_iota(jnp.int32, (t, t), 0)).astype(BF16)
    c = _split3_dot(tri, logf) + carry_ref[0:1, :]
    o_ref[0] = c * LOG2_E
    carry_ref[0:1, :] = c[t - 1:t, :]


def _gate_cumsum(lora3, bias_row, col_block, t):
    b, s, _ = lora3.shape
    t = min(t, s)
    return pl.pallas_call(
        _gate_kernel,
        grid=(b, s // t),
        in_specs=[pl.BlockSpec((1, t, LANES), lambda bi, i: (bi, i, col_block)),
                  pl.BlockSpec((1, LANES), lambda bi, i: (0, 0))],
        out_specs=pl.BlockSpec((1, t, LANES), lambda bi, i: (bi, i, 0)),
        out_shape=jax.ShapeDtypeStruct((b, s, LANES), F32),
        name="fox_gate_cumsum",
        scratch_shapes=[pltpu.VMEM((8, LANES), F32)],
        compiler_params=pltpu.CompilerParams(dimension_semantics=("parallel", "arbitrary")),
    )(lora3, bias_row)


def _fox_kernel(q_ref, k_ref, v_ref, c_ref, z_ref, o_ref, kaug_ref, vt_ref, *unit_refs, tq, c_lane0, group_cols):
    i = pl.program_id(2)
    s_len = k_ref.shape[1]
    dh = FOX_HEAD_DIM
    n_heads = k_ref.shape[2] // dh
    gw = group_cols
    n_groups = tq // gw
    units = [(hh, g) for hh in range(n_heads) for g in range(n_groups)]
    nu = len(units)
    sa, sb, pa, pb, m_refs, al_refs, acc_refs = (unit_refs[kind * nu:(kind + 1) * nu] for kind in range(7))

    @pl.when(i == 0)
    def _():
        lane_r = lax.broadcasted_iota(jnp.int32, (LANES, LANES), 0)
        lane_c = lax.broadcasted_iota(jnp.int32, (LANES, LANES), 1)
        dot = functools.partial(jnp.dot, preferred_element_type=F32)
        for hh in range(n_heads):
            c_lane = c_lane0 + pl.program_id(1) * n_heads + hh
            pick = [jnp.where((lane_r == c_lane) & (lane_c == term), 1.0, 0.0).astype(BF16) for term in range(3)]
            hs = slice(hh * dh, (hh + 1) * dh)
            for jb in range(s_len // tq):
                sl = slice(jb * tq, (jb + 1) * tq)
                c = c_ref[0, sl, :]
                hi = c.astype(BF16)
                r1 = c - hi.astype(F32)
                mid = r1.astype(BF16)
                lo = (r1 - mid.astype(F32)).astype(BF16)
                c_terms = dot(hi, pick[0]) + dot(mid, pick[1]) + dot(lo, pick[2])
                kaug_ref[hh, sl, :dh] = k_ref[0, sl, hs]
                kaug_ref[hh, sl, dh:] = (-c_terms).astype(BF16)
                vt_ref[hh, jb, :dh, :] = v_ref[0, sl, hs].astype(F32).T.astype(BF16)
                vt_ref[hh, jb, dh:, :] = jnp.ones((vt_ref.shape[2] - dh, tq), BF16)

    ones3 = jnp.where(lax.broadcasted_iota(jnp.int32, (dh, tq), 0) < 3, 1.0, 0.0)
    q_t = [jnp.concatenate([q_ref[0, :, hh * dh:(hh + 1) * dh].astype(F32).T, ones3], axis=0).astype(BF16)
           for hh in range(n_heads)]
    q_u = [q_t[hh][:, g * gw:(g + 1) * gw] for (hh, g) in units]

    def scores(j, s_out):
        rows = pl.ds(pl.multiple_of(j * tq, tq), tq)
        for u, (hh, _) in enumerate(units):
            s_out[u][...] = jnp.dot(kaug_ref[hh, rows, :], q_u[u], preferred_element_type=F32)

    def accumulate(p, j):
        for u, (hh, _) in enumerate(units):
            acc_refs[u][...] = (al_refs[u][...] * acc_refs[u][...]
                                + jnp.dot(vt_ref[hh, j], p[u], preferred_element_type=F32))

    def softmax_step(s):
        p = []
        for u in range(nu):
            m = m_refs[u][...]
            m_new = jnp.maximum(m, jnp.max(s[u], axis=0, keepdims=True))
            p.append(jnp.exp2(s[u] - m_new).astype(BF16))
            al_refs[u][...] = jnp.exp2(m - m_new)
            m_refs[u][...] = m_new
        return p

    def step(j, s_in, s_out, p_in, p_out):
        scores(j + 1, s_out)
        accumulate([p_in[u][...] for u in range(nu)], jnp.maximum(j - 1, 0))
        p = softmax_step([s_in[u][...] for u in range(nu)])
        for u in range(nu):
            p_out[u][...] = p[u]

    scores(0, sa)
    for u in range(nu):
        pa[u][...] = jnp.zeros_like(pa[u])
        m_refs[u][...] = jnp.full(m_refs[u].shape, -jnp.inf, F32)
        al_refs[u][...] = jnp.ones_like(al_refs[u])
        acc_refs[u][...] = jnp.zeros_like(acc_refs[u])

    def pair(t, carry):
        step(2 * t, sa, sb, pa, pb)
        step(2 * t + 1, sb, sa, pb, pa)
        return carry

    lax.fori_loop(0, i // 2, pair, 0)

    def finish(s_last, p_last):
        accumulate([p_last[u][...] for u in range(nu)], jnp.maximum(i - 1, 0))
        key = lax.broadcasted_iota(jnp.int32, (tq, gw), 0)
        qry = lax.broadcasted_iota(jnp.int32, (tq, gw), 1)
        p = softmax_step([jnp.where(key <= qry + g * gw, s_last[u][...], -jnp.inf)
                          for u, (_, g) in enumerate(units)])
        accumulate(p, i)
        for u, (hh, g) in enumerate(units):
            acc_u = acc_refs[u][...]
            gate = z_ref[0, g * gw:(g + 1) * gw, hh * dh:(hh + 1) * dh].astype(F32)
            out = (acc_u[:dh, :] / acc_u[dh:dh + 1, :]).T
            o_ref[0, g * gw:(g + 1) * gw, hh * dh:(hh + 1) * dh] = (out * gate).astype(o_ref.dtype)

    @pl.when(i % 2 == 1)
    def _():
        step(i - 1, sa, sb, pa, pb)
        finish(sb, pb)

    @pl.when(i % 2 == 0)
    def _():
        finish(sa, pa)


def _fox_attention(qkvz, c2, n_heads, tq, c_lane0, heads_per_step):
    b, s, _ = qkvz.shape
    nq = s // tq
    nh = heads_per_step
    hb = n_heads // nh
    assert hb * nh == n_heads
    w = nh * FOX_HEAD_DIM
    acc_rows = FOX_HEAD_DIM + BF16_SUBLANES
    gw = min(256, tq)
    n_units = nh * (tq // gw)
    return pl.pallas_call(
        functools.partial(_fox_kernel, tq=tq, c_lane0=c_lane0, group_cols=gw),
        grid=(b, hb, nq),
        in_specs=[pl.BlockSpec((1, tq, w), lambda bi, hi, i: (bi, i, hi)),
                  pl.BlockSpec((1, s, w), lambda bi, hi, i: (bi, 0, hb + hi)),
                  pl.BlockSpec((1, s, w), lambda bi, hi, i: (bi, 0, 2 * hb + hi)),
                  pl.BlockSpec((1, s, LANES), lambda bi, hi, i: (bi, 0, 0)),
                  pl.BlockSpec((1, tq, w), lambda bi, hi, i: (bi, i, 3 * hb + hi))],
        out_specs=pl.BlockSpec((1, tq, w), lambda bi, hi, i: (bi, i, hi)),
        out_shape=jax.ShapeDtypeStruct((b, s, n_heads * FOX_HEAD_DIM), BF16),
        name="fox_attention",
        scratch_shapes=[pltpu.VMEM((nh, s, 2 * FOX_HEAD_DIM), BF16), pltpu.VMEM((nh, nq, acc_rows, tq), BF16)]
        + [pltpu.VMEM((tq, gw), F32)] * (2 * n_units)
        + [pltpu.VMEM((tq, gw), BF16)] * (2 * n_units)
        + [pltpu.VMEM((1, gw), F32)] * (2 * n_units)
        + [pltpu.VMEM((acc_rows, gw), F32)] * n_units,
        compiler_params=pltpu.CompilerParams(
            dimension_semantics=("parallel", "parallel", "arbitrary"), vmem_limit_bytes=VMEM_LIMIT_BYTES),
    )(qkvz, qkvz, qkvz, c2, qkvz)


def _stack_heads(x, head0):
    zero = jnp.zeros_like(x)
    return jnp.concatenate([jnp.where(head0, x, zero), jnp.where(head0, zero, x)], axis=0)


def _rwkv_chunks(units, masks):
    head0, strict, diag, blockdiag, eye = masks
    incl = strict | diag
    c = units[0]["r"].shape[0]
    n_units = range(len(units))
    zero = jnp.zeros((c, LANES), F32)
    nt = (((1,), (1,)), ((), ()))
    tn = (((0,), (0,)), ((), ()))

    rt, at, bh, kh, gc, sv, lhs1, rhs1 = [], [], [], [], [], [], [], []
    for u in units:
        lg = u["lg"]
        lgc = lg[c - 1:c, :]
        e_neg = jnp.exp2(-lg)
        e_end = jnp.exp2(lgc - lg)
        rt.append(u["r"] * jnp.exp2(lg))
        at.append(u["av"] * jnp.exp2(lg - u["logw"]))
        bh.append(u["bv"] * e_end)
        kh.append(u["k2"] * e_end)
        gc.append(jnp.exp2(lgc))
        sv.append(_stack_heads(u["v"], head0))
        lhs1.append(jnp.concatenate([at[-1], rt[-1]], axis=0))
        rhs1.append(jnp.concatenate([_stack_heads(u["bv"] * e_neg, head0),
                                     _stack_heads(u["k2"] * e_neg, head0)], axis=0))

    big = [_bdot(lhs1[i], rhs1[i], nt) for i in n_units]
    l_mat = [jnp.where(strict, big[i][:c, :LANES], zero) for i in n_units]
    a_ak = [jnp.where(strict, big[i][:c, LANES:], zero) for i in n_units]
    a_r = [jnp.concatenate([jnp.where(incl, big[i][c:, :LANES], zero),
                            jnp.where(incl, big[i][c:, LANES:], zero)], axis=1) for i in n_units]

    aakv = [_bdot(a_ak[i], sv[i]) for i in n_units]
    n_steps = max(1, (c - 1).bit_length())
    t_inv = [jnp.where(diag, 1.0, 0.0) + l_mat[i] for i in n_units]
    cur = [_bdot(l_mat[i], _stack_heads(l_mat[i], head0)) for i in n_units] if n_steps > 1 else None
    for step in range(1, n_steps):
        if step + 1 < n_steps:
            both = [_bdot(jnp.concatenate([cur[i], t_inv[i]], axis=0), _stack_heads(cur[i], head0)) for i in n_units]
            t_inv = [t_inv[i] + both[i][c:] for i in n_units]
            cur = [both[i][:c] for i in n_units]
        else:
            t_inv = [t_inv[i] + _bdot(t_inv[i], _stack_heads(cur[i], head0)) for i in n_units]
    x = [_bdot(t_inv[i], jnp.concatenate([_stack_heads(at[i], head0), _stack_heads(aakv[i], head0)], axis=1))
         for i in n_units]

    rhs2 = [jnp.concatenate(
        [jnp.concatenate([_stack_heads(x[i][:, :LANES], head0), _stack_heads(x[i][:, LANES:], head0)], axis=1),
         jnp.concatenate([jnp.zeros((2 * c, LANES), F32), sv[i]], axis=1)], axis=0) for i in n_units]
    qy = [_bdot(a_r[i], rhs2[i]) for i in n_units]
    gh = [_bdot(jnp.concatenate([bh[i], kh[i]], axis=0),
                jnp.concatenate([x[i], jnp.concatenate([zero, units[i]["v"]], axis=1)], axis=0), tn)
          for i in n_units]
    out = []
    for i in n_units:
        out.append((rt[i] + qy[i][:, :LANES], qy[i][:, LANES:],
                    jnp.where(blockdiag, gh[i][:, :LANES], 0.0) + eye * gc[i],
                    jnp.where(blockdiag, gh[i][:, LANES:], 0.0)))
    return out


def _rwkv_kernel(r_ref, k_ref, v_ref, lo_ref, z_ref, wup_ref, aup_ref, w0_ref, a0_ref, kk_ref, ka_ref, rk_ref,
                 gg_ref, gb_ref, o_ref, sbd_ref):
    nb, t, w = r_ref.shape
    c = RWKV_CHUNK
    n = RWKV_HEAD_DIM
    n_pairs = w // LANES
    n_chunks = t // c

    @pl.when(pl.program_id(1) == 0)
    def _():
        sbd_ref[...] = jnp.zeros_like(sbd_ref)

    wr = lax.broadcasted_iota(jnp.int32, (w, w), 0)
    wc = lax.broadcasted_iota(jnp.int32, (w, w), 1)
    head_ones = ((wr // n) == (wc // n)).astype(BF16)
    gr = min(t, 2 * c)
    tr = lax.broadcasted_iota(jnp.int32, (gr, gr), 0)
    tc = lax.broadcasted_iota(jnp.int32, (gr, gr), 1)
    chunk_tri = ((tc <= tr) & ((tc // c) == (tr // c))).astype(BF16)

    def chunk_cumsum(x):
        return jnp.concatenate([_split2_dot(chunk_tri, x[g0:g0 + gr]) for g0 in range(0, t, gr)], axis=0)

    def head_sum(x):
        return _bdot(x, head_ones)

    prepped = []
    for bi in range(nb):
        r, k, v, lora = r_ref[bi], k_ref[bi], v_ref[bi], lo_ref[bi]
        dec = _bdot(jnp.tanh(lora), wup_ref[...])
        logw = -(EXP_NEG_HALF * LOG2_E) * _sigmoid(w0_ref[...] + dec)
        a = _sigmoid(a0_ref[...] + _bdot(lora, aup_ref[...]))
        kk = k * kk_ref[...]
        kk = kk * lax.rsqrt(jnp.maximum(head_sum(kk * kk), 1e-24))
        k2 = k * (1.0 + (a - 1.0) * ka_ref[...])
        prepped.append(dict(r=r, k2=k2, v=v, av=-kk, bv=kk * a, logw=logw, lg=chunk_cumsum(logw)))

    lane = lax.broadcasted_iota(jnp.int32, (c, LANES), 1)
    row = lax.broadcasted_iota(jnp.int32, (c, LANES), 0)
    head0 = lane < n
    col = jnp.where(head0, lane, lane - n)
    lane_r = lax.broadcasted_iota(jnp.int32, (LANES, LANES), 0)
    lane_c = lax.broadcasted_iota(jnp.int32, (LANES, LANES), 1)
    masks = (head0, col < row, col == row, (lane_r < n) == (lane_c < n), (lane_r == lane_c).astype(F32))

    seqs = [(bi, pi) for bi in range(nb) for pi in range(n_pairs)]
    units = [{name: arr[ci * c:(ci + 1) * c, pi * LANES:(pi + 1) * LANES] for name, arr in prepped[bi].items()}
             for ci in range(n_chunks) for (bi, pi) in seqs]
    affine = _rwkv_chunks(units, masks)

    sbd = [sbd_ref[si] for si in range(len(seqs))]
    ys = {}
    for ci in range(n_chunks):
        qs = [affine[ci * len(seqs) + si] for si in range(len(seqs))]
        both = [_bdot(jnp.concatenate([qs[si][0], qs[si][2]], axis=0), sbd[si]) for si in range(len(seqs))]
        sbd = [both[si][c:] + qs[si][3] for si in range(len(seqs))]
        for si, seq in enumerate(seqs):
            ys[seq + (ci,)] = both[si][:c] + qs[si][1]
    for si in range(len(seqs)):
        sbd_ref[si] = sbd[si]

    for bi in range(nb):
        y = jnp.concatenate([jnp.concatenate([ys[(bi, pi, ci)] for pi in range(n_pairs)], axis=1)
                             for ci in range(n_chunks)], axis=0)
        p = prepped[bi]
        mu = head_sum(y) * (1.0 / n)
        d = y - mu
        var = head_sum(d * d) * (1.0 / n)
        y = d * lax.rsqrt(var + GN_EPS) * gg_ref[...] + gb_ref[...]
        y = y + head_sum(p["r"] * p["k2"] * rk_ref[...]) * p["v"]
        o_ref[bi] = (y * z_ref[bi].astype(F32)).astype(o_ref.dtype)


def _rwkv_mix(rkvl, qkvz, rows, wup_pad, aup_pad, n_pairs, lora_cols, z_col0, t, pairs_per_step):
    b, s, _ = rkvl.shape
    t = min(t, s)
    w = pairs_per_step * LANES
    p = n_pairs // pairs_per_step
    assert p * pairs_per_step == n_pairs and z_col0 % w == 0
    lora_block = 3 * n_pairs * LANES // lora_cols
    assert lora_block * lora_cols == 3 * n_pairs * LANES
    tok = lambda off: pl.BlockSpec((b, t, w), lambda j, i: (0, i, off + j))
    prow = lambda off: pl.BlockSpec((1, w), lambda j, i: (0, off + j))
    in_specs = [
        tok(0), tok(p), tok(2 * p),
        pl.BlockSpec((b, t, lora_cols), lambda j, i: (0, i, lora_block)),
        tok(z_col0 // w),
        pl.BlockSpec((lora_cols, w), lambda j, i: (0, j)),
        pl.BlockSpec((lora_cols, w), lambda j, i: (0, j)),
    ] + [prow(0)] * 7
    return pl.pallas_call(
        _rwkv_kernel,
        grid=(p, s // t),
        in_specs=in_specs,
        out_specs=pl.BlockSpec((b, t, w), lambda j, i: (0, i, j)),
        out_shape=jax.ShapeDtypeStruct((b, s, n_pairs * LANES), BF16),
        name="rwkv7_mix",
        scratch_shapes=[pltpu.VMEM((b * pairs_per_step, LANES, LANES), F32)],
        compiler_params=pltpu.CompilerParams(
            dimension_semantics=("parallel", "arbitrary"), vmem_limit_bytes=VMEM_LIMIT_BYTES),
    )(rkvl, rkvl, rkvl, rkvl, qkvz, wup_pad, aup_pad,
      rows["w0"], rows["a0"], rows["k_k"], rows["k_a"], rows["r_k"], rows["gn_gain"], rows["gn_bias"])


def _out_kernel(hf_ref, hr_ref, wt_ref, wb_ref, x_ref, g_ref, b_ref, o_ref, acc_ref, sum_ref, *, alpha, n_tiles):
    j = pl.program_id(1)
    out = (jnp.dot(hf_ref[...], wt_ref[...], preferred_element_type=F32)
           + jnp.dot(hr_ref[...], wb_ref[...], preferred_element_type=F32))
    pre = alpha * x_ref[...] + out
    acc_ref[j] = pre
    part = jnp.sum(pre, axis=-1, keepdims=True)
    sum_ref[...] = jnp.where(j == 0, part, sum_ref[...] + part)

    @pl.when(j == n_tiles - 1)
    def _():
        tn = acc_ref.shape[2]
        d = n_tiles * tn
        mean = sum_ref[...] * (1.0 / d)
        sq = jnp.sum(jnp.square(acc_ref[0] - mean), axis=-1, keepdims=True)
        for jj in range(1, n_tiles):
            sq = sq + jnp.sum(jnp.square(acc_ref[jj] - mean), axis=-1, keepdims=True)
        inv = lax.rsqrt(sq * (1.0 / d) + LN_EPS)
        for jj in range(n_tiles):
            sl = slice(jj * tn, (jj + 1) * tn)
            o_ref[:, sl] = ((acc_ref[jj] - mean) * inv * g_ref[:, sl] + b_ref[:, sl]).astype(o_ref.dtype)


def _out_proj_layernorm(hf, hr, w_out_bf16, x2, ln_gain, ln_bias, alpha, tm, tn):
    m, d = x2.shape
    kf, kr = hf.shape[1], hr.shape[1]
    assert kf == kr and kf + kr == w_out_bf16.shape[0]
    tm, tn = min(tm, m), min(tn, d)
    n_tiles = d // tn
    return pl.pallas_call(
        functools.partial(_out_kernel, alpha=alpha, n_tiles=n_tiles),
        grid=(m // tm, n_tiles),
        in_specs=[pl.BlockSpec((tm, kf), lambda i, j: (i, 0)),
                  pl.BlockSpec((tm, kr), lambda i, j: (i, 0)),
                  pl.BlockSpec((kf, tn), lambda i, j: (0, j)),
                  pl.BlockSpec((kr, tn), lambda i, j: (1, j)),
                  pl.BlockSpec((tm, tn), lambda i, j: (i, j)),
                  pl.BlockSpec((1, d), lambda i, j: (0, 0)),
                  pl.BlockSpec((1, d), lambda i, j: (0, 0))],
        out_specs=pl.BlockSpec((tm, d), lambda i, j: (i, 0)),
        out_shape=jax.ShapeDtypeStruct((m, d), x2.dtype),
        name="out_proj_layernorm",
        scratch_shapes=[pltpu.VMEM((n_tiles, tm, tn), F32), pltpu.VMEM((tm, 1), F32)],
        compiler_params=pltpu.CompilerParams(
            dimension_semantics=("parallel", "arbitrary"), vmem_limit_bytes=VMEM_LIMIT_BYTES),
    )(hf, hr, w_out_bf16, w_out_bf16, x2, ln_gain.reshape(1, d).astype(F32), ln_bias.reshape(1, d).astype(F32))


def kernel(x, w_in, f_bias, mu_shift, w0, w_up, a0, a_up, k_k, k_a, r_k, gn_gain, gn_bias, w_out, ln_gain, ln_bias):
    b, s, d = x.shape
    fox_heads = f_bias.shape[0]
    fw = fox_heads * FOX_HEAD_DIM
    rw = w0.shape[0]
    n_pairs = rw // LANES
    assert r_k.shape[1] == RWKV_HEAD_DIM and rw % LANES == 0
    dl, al = w_up.shape[0], a_up.shape[0]
    mix = fw + rw
    assert w_in.shape[1] == 3 * fw + fox_heads + 3 * rw + dl + al + mix
    depth = 1
    alpha = (2 * depth) ** 0.25

    o_ff = 3 * fw
    o_r = o_ff + fox_heads
    o_wd = o_r + 3 * rw
    o_ad = o_wd + dl
    o_z = o_ad + al

    lora_cols = -(-(dl + al) // 256) * 256
    assert (3 * rw) % lora_cols == 0 and fox_heads <= LANES
    assert o_r + 3 * rw + lora_cols <= w_in.shape[1] and o_ff + LANES <= w_in.shape[1]
    assert all(off % BF16_SUBLANES == 0 for off in (o_ff, o_r, o_z)), "weight row windows must be tile aligned"

    x2 = x.reshape(b * s, d)
    w_t = w_in.T.astype(BF16)
    tn_bf = _pick_tile(math.gcd(fw, mix), 1024)
    tn_f32 = _pick_tile(3 * rw + lora_cols, 1280)
    mu_row = jnp.concatenate([mu_shift.astype(F32), jnp.zeros((lora_cols - dl - al,), F32)]).reshape(1, -1)

    xb = x2.astype(BF16)
    qkvz = _in_proj_fox(xb, w_t, 1024, tn_bf, fw // tn_bf, 3 * fw // tn_bf, o_z, mix // tn_bf,
                        LOG2_E * FOX_HEAD_DIM ** -0.5).reshape(b, s, 3 * fw + mix)
    rkvl, ff = _in_proj_rwkv(xb, w_t, mu_row, 512, tn_f32, o_r, (3 * rw + lora_cols) // tn_f32, o_ff, s)
    rkvl = rkvl.reshape(b, s, 3 * rw + lora_cols)

    bias_row = jnp.zeros((1, LANES), F32).at[0, :fox_heads].set(f_bias.astype(F32))
    c2 = _gate_cumsum(ff.reshape(b, s, LANES), bias_row, 0, 512)
    hf = _fox_attention(qkvz, c2, fox_heads, min(512, s), 0, 2)

    pad_rows = lambda w, off: jnp.zeros((lora_cols, rw), F32).at[off:off + w.shape[0]].set(w.astype(F32)).astype(BF16)
    rows = {}
    for name, val in (("w0", w0), ("a0", a0), ("k_k", k_k), ("k_a", k_a), ("r_k", r_k),
                      ("gn_gain", gn_gain), ("gn_bias", gn_bias)):
        rows[name] = val.astype(F32).reshape(1, rw)
    hr = _rwkv_mix(rkvl, qkvz, rows, pad_rows(w_up, 0), pad_rows(a_up, dl), n_pairs, lora_cols,
                   3 * fw + fw, 512, 2)

    out = _out_proj_layernorm(hf.reshape(b * s, fw), hr.reshape(b * s, rw), w_out.astype(BF16),
                              x2, ln_gain, ln_bias, alpha, 512, 512)
    return out.reshape(b, s, d)
```

```python
import functools
import math

import jax
import jax.numpy as jnp
from jax import lax
from jax.experimental import pallas as pl
from jax.experimental.pallas import tpu as pltpu

F32 = jnp.float32
BF16 = jnp.bfloat16

LANES = 128
BF16_SUBLANES = 16
FOX_HEAD_DIM = 128
RWKV_HEAD_DIM = 64
RWKV_CHUNK = 64
LOG2_E = 1.4426950408889634
EXP_NEG_HALF = 0.6065306597126334
LN_EPS = 1e-5
GN_EPS = 64e-5
VMEM_LIMIT_BYTES = 56 * 1024 * 1024


def _bdot(a, b, dims=(((1,), (0,)), ((), ()))):
    return lax.dot_general(a.astype(BF16), b.astype(BF16), dims, preferred_element_type=F32)


def _sigmoid(x):
    return 0.5 * jnp.tanh(0.5 * x) + 0.5


def _split3_dot(m_bf16, x):
    hi = x.astype(BF16)
    r1 = x - hi.astype(F32)
    mid = r1.astype(BF16)
    lo = (r1 - mid.astype(F32)).astype(BF16)
    dot = functools.partial(jnp.dot, preferred_element_type=F32)
    return dot(m_bf16, hi) + dot(m_bf16, mid) + dot(m_bf16, lo)


def _split2_dot(m_bf16, x):
    hi = x.astype(BF16)
    lo = (x - hi.astype(F32)).astype(BF16)
    dot = functools.partial(jnp.dot, preferred_element_type=F32)
    return dot(m_bf16, hi) + dot(m_bf16, lo)


def _pick_tile(n, preferred):
    t = min(preferred, n) // LANES * LANES
    while n % t:
        t -= LANES
    return t


_NT = (((1,), (1,)), ((), ()))


def _proj_fox_kernel(a_ref, wt_ref, o_ref, *, n_q_blocks, n_qkv_blocks, q_scale):
    j = pl.program_id(1)
    acc = lax.dot_general(a_ref[...], wt_ref[...], _NT, preferred_element_type=F32)
    factor = jnp.where(j >= n_qkv_blocks, _sigmoid(acc), jnp.where(j < n_q_blocks, q_scale, 1.0))
    o_ref[...] = (acc * factor).astype(o_ref.dtype)


def _proj_rwkv_kernel(a_ref, wt_ref, et_ref, mu_ref, o_ref, e_ref, last_ref, *, blocks_per_seq):
    i, j = pl.program_id(0), pl.program_id(1)

    @pl.when(j == 0)
    def _():
        e_ref[...] = lax.dot_general(a_ref[...], et_ref[...], _NT, preferred_element_type=F32)

    p = lax.dot_general(a_ref[...], wt_ref[...], _NT, preferred_element_type=F32)
    tm = p.shape[0]
    above = jnp.where(i % blocks_per_seq == 0, 0.0, last_ref[j])
    first_row = lax.broadcasted_iota(jnp.int32, (tm, 1), 0) == 0
    prev = jnp.where(first_row, above, pltpu.roll(p, 1, axis=0))
    last_ref[j] = p[tm - 1:tm, :]
    o_ref[...] = p + (prev - p) * mu_ref[...]


def _w_row_map(segments, tn):
    def w_row(j):
        row, first = None, 0
        for nb, row0 in segments:
            here = row0 + (j - first) * tn
            row = here if row is None else jnp.where(j >= first, here, row)
            first += nb
        return pl.multiple_of(row, BF16_SUBLANES)
    return w_row


def _in_proj_fox(a, w_t, tm, tn, n_q_blocks, n_qkv_blocks, z_row0, n_z_blocks, q_scale):
    m, k = a.shape
    tm = _pick_tile(m, tm)
    w_row = _w_row_map([(n_qkv_blocks, 0), (n_z_blocks, z_row0)], tn)
    n_blocks = n_qkv_blocks + n_z_blocks
    return pl.pallas_call(
        functools.partial(_proj_fox_kernel, n_q_blocks=n_q_blocks, n_qkv_blocks=n_qkv_blocks, q_scale=q_scale),
        grid=(m // tm, n_blocks),
        in_specs=[pl.BlockSpec((tm, k), lambda i, j: (i, 0)),
                  pl.BlockSpec((pl.Element(tn), pl.Element(k)), lambda i, j: (w_row(j), 0))],
        out_specs=pl.BlockSpec((tm, tn), lambda i, j: (i, j)),
        out_shape=jax.ShapeDtypeStruct((m, n_blocks * tn), BF16),
        name="in_proj_fox",
        compiler_params=pltpu.CompilerParams(
            dimension_semantics=("parallel", "arbitrary"), vmem_limit_bytes=VMEM_LIMIT_BYTES),
    )(a, w_t)


def _in_proj_rwkv(a, w_t, mu_row, tm, tn, row0, n_blocks, gate_row0, seq_len):
    m, k = a.shape
    tm = _pick_tile(math.gcd(m, seq_len), tm)
    w_row = _w_row_map([(n_blocks, row0)], tn)
    return pl.pallas_call(
        functools.partial(_proj_rwkv_kernel, blocks_per_seq=seq_len // tm),
        grid=(m // tm, n_blocks),
        in_specs=[pl.BlockSpec((tm, k), lambda i, j: (i, 0)),
                  pl.BlockSpec((pl.Element(tn), pl.Element(k)), lambda i, j: (w_row(j), 0)),
                  pl.BlockSpec((pl.Element(LANES), pl.Element(k)), lambda i, j: (gate_row0, 0)),
                  pl.BlockSpec((1, tn), lambda i, j: (0, j))],
        out_specs=[pl.BlockSpec((tm, tn), lambda i, j: (i, j)),
                   pl.BlockSpec((tm, LANES), lambda i, j: (i, 0))],
        out_shape=[jax.ShapeDtypeStruct((m, n_blocks * tn), F32), jax.ShapeDtypeStruct((m, LANES), F32)],
        name="in_proj_rwkv",
        scratch_shapes=[pltpu.VMEM((n_blocks, 1, tn), F32)],
        compiler_params=pltpu.CompilerParams(
            dimension_semantics=("arbitrary", "arbitrary"), vmem_limit_bytes=VMEM_LIMIT_BYTES),
    )(a, w_t, w_t, mu_row)


def _gate_kernel(x_ref, b_ref, o_ref, carry_ref):
    @pl.when(pl.program_id(1) == 0)
    def _():
        carry_ref[...] = jnp.zeros_like(carry_ref)

    t = x_ref.shape[1]
    x = x_ref[0] + b_ref[...]
    logf = jnp.minimum(x, 0.0) - jnp.log1p(jnp.exp(-jnp.abs(x)))
    tri = (lax.broadcasted_iota(jnp.int32, (t, t), 1)
           <= lax.broadcasted_iota(jnp.int32, (t, t), 0)).astype(BF16)
    c = _split3_dot(tri, logf) + carry_ref[0:1, :]
    o_ref[0] = c * LOG2_E
    carry_ref[0:1, :] = c[t - 1:t, :]


def _gate_cumsum(lora3, bias_row, col_block, t):
    b, s, _ = lora3.shape
    t = min(t, s)
    return pl.pallas_call(
        _gate_kernel,
        grid=(b, s // t),
        in_specs=[pl.BlockSpec((1, t, LANES), lambda bi, i: (bi, i, col_block)),
                  pl.BlockSpec((1, LANES), lambda bi, i: (0, 0))],
        out_specs=pl.BlockSpec((1, t, LANES), lambda bi, i: (bi, i, 0)),
        out_shape=jax.ShapeDtypeStruct((b, s, LANES), F32),
        name="fox_gate_cumsum",
        scratch_shapes=[pltpu.VMEM((8, LANES), F32)],
        compiler_params=pltpu.CompilerParams(dimension_semantics=("parallel", "arbitrary")),
    )(lora3, bias_row)


def _fox_kernel(q_ref, k_ref, v_ref, c_ref, z_ref, o_ref, kaug_ref, vt_ref, *unit_refs, tq, c_lane0, group_cols):
    i = pl.program_id(2)
    s_len = k_ref.shape[1]
    dh = FOX_HEAD_DIM
    n_heads = k_ref.shape[2] // dh
    gw = group_cols
    n_groups = tq // gw
    units = [(hh, g) for hh in range(n_heads) for g in range(n_groups)]
    nu = len(units)
    sa, sb, pa, pb, m_refs, al_refs, acc_refs = (unit_refs[kind * nu:(kind + 1) * nu] for kind in range(7))

    @pl.when(i == 0)
    def _():
        lane_r = lax.broadcasted_iota(jnp.int32, (LANES, LANES), 0)
        lane_c = lax.broadcasted_iota(jnp.int32, (LANES, LANES), 1)
        dot = functools.partial(jnp.dot, preferred_element_type=F32)
        for hh in range(n_heads):
            c_lane = c_lane0 + pl.program_id(1) * n_heads + hh
            pick = [jnp.where((lane_r == c_lane) & (lane_c == term), 1.0, 0.0).astype(BF16) for term in range(3)]
            hs = slice(hh * dh, (hh + 1) * dh)
            for jb in range(s_len // tq):
                sl = slice(jb * tq, (jb + 1) * tq)
                c = c_ref[0, sl, :]
                hi = c.astype(BF16)
                r1 = c - hi.astype(F32)
                mid = r1.astype(BF16)
                lo = (r1 - mid.astype(F32)).astype(BF16)
                c_terms = dot(hi, pick[0]) + dot(mid, pick[1]) + dot(lo, pick[2])
                kaug_ref[hh, sl, :dh] = k_ref[0, sl, hs]
                kaug_ref[hh, sl, dh:] = (-c_terms).astype(BF16)
                vt_ref[hh, jb, :dh, :] = v_ref[0, sl, hs].astype(F32).T.astype(BF16)
                vt_ref[hh, jb, dh:, :] = jnp.ones((vt_ref.shape[2] - dh, tq), BF16)

    ones3 = jnp.where(lax.broadcasted_iota(jnp.int32, (dh, tq), 0) < 3, 1.0, 0.0)
    q_t = [jnp.concatenate([q_ref[0, :, hh * dh:(hh + 1) * dh].astype(F32).T, ones3], axis=0).astype(BF16)
           for hh in range(n_heads)]
    q_u = [q_t[hh][:, g * gw:(g + 1) * gw] for (hh, g) in units]

    def scores(j, s_out):
        rows = pl.ds(pl.multiple_of(j * tq, tq), tq)
        for u, (hh, _) in enumerate(units):
            s_out[u][...] = jnp.dot(kaug_ref[hh, rows, :], q_u[u], preferred_element_type=F32)

    def accumulate(p, j):
        for u, (hh, _) in enumerate(units):
            acc_refs[u][...] = (al_refs[u][...] * acc_refs[u][...]
                                + jnp.dot(vt_ref[hh, j], p[u], preferred_element_type=F32))

    def softmax_step(s):
        p = []
        for u in range(nu):
            m = m_refs[u][...]
            m_new = jnp.maximum(m, jnp.max(s[u], axis=0, keepdims=True))
            p.append(jnp.exp2(s[u] - m_new).astype(BF16))
            al_refs[u][...] = jnp.exp2(m - m_new)
            m_refs[u][...] = m_new
        return p

    def step(j, s_in, s_out, p_in, p_out):
        scores(j + 1, s_out)
        accumulate([p_in[u][...] for u in range(nu)], jnp.maximum(j - 1, 0))
        p = softmax_step([s_in[u][...] for u in range(nu)])
        for u in range(nu):
            p_out[u][...] = p[u]

    scores(0, sa)
    for u in range(nu):
        pa[u][...] = jnp.zeros_like(pa[u])
        m_refs[u][...] = jnp.full(m_refs[u].shape, -jnp.inf, F32)
        al_refs[u][...] = jnp.ones_like(al_refs[u])
        acc_refs[u][...] = jnp.zeros_like(acc_refs[u])

    def pair(t, carry):
        step(2 * t, sa, sb, pa, pb)
        step(2 * t + 1, sb, sa, pb, pa)
        return carry

    lax.fori_loop(0, i // 2, pair, 0)

    def finish(s_last, p_last):
        accumulate([p_last[u][...] for u in range(nu)], jnp.maximum(i - 1, 0))
        key = lax.broadcasted_iota(jnp.int32, (tq, gw), 0)
        qry = lax.broadcasted_iota(jnp.int32, (tq, gw), 1)
        p = softmax_step([jnp.where(key <= qry + g * gw, s_last[u][...], -jnp.inf)
                          for u, (_, g) in enumerate(units)])
        accumulate(p, i)
        for u, (hh, g) in enumerate(units):
            acc_u = acc_refs[u][...]
            gate = z_ref[0, g * gw:(g + 1) * gw, hh * dh:(hh + 1) * dh].astype(F32)
            out = (acc_u[:dh, :] / acc_u[dh:dh + 1, :]).T
            o_ref[0, g * gw:(g + 1) * gw, hh * dh:(hh + 1) * dh] = (out * gate).astype(o_ref.dtype)

    @pl.when(i % 2 == 1)
    def _():
        step(i - 1, sa, sb, pa, pb)
        finish(sb, pb)

    @pl.when(i % 2 == 0)
    def _():
        finish(sa, pa)


def _fox_attention(qkvz, c2, n_heads, tq, c_lane0, heads_per_step):
    b, s, _ = qkvz.shape
    nq = s // tq
    nh = heads_per_step
    hb = n_heads // nh
    assert hb * nh == n_heads
    w = nh * FOX_HEAD_DIM
    acc_rows = FOX_HEAD_DIM + BF16_SUBLANES
    gw = min(256, tq)
    n_units = nh * (tq // gw)
    return pl.pallas_call(
        functools.partial(_fox_kernel, tq=tq, c_lane0=c_lane0, group_cols=gw),
        grid=(b, hb, nq),
        in_specs=[pl.BlockSpec((1, tq, w), lambda bi, hi, i: (bi, i, hi)),
                  pl.BlockSpec((1, s, w), lambda bi, hi, i: (bi, 0, hb + hi)),
                  pl.BlockSpec((1, s, w), lambda bi, hi, i: (bi, 0, 2 * hb + hi)),
                  pl.BlockSpec((1, s, LANES), lambda bi, hi, i: (bi, 0, 0)),
                  pl.BlockSpec((1, tq, w), lambda bi, hi, i: (bi, i, 3 * hb + hi))],
        out_specs=pl.BlockSpec((1, tq, w), lambda bi, hi, i: (bi, i, hi)),
        out_shape=jax.ShapeDtypeStruct((b, s, n_heads * FOX_HEAD_DIM), BF16),
        name="fox_attention",
        scratch_shapes=[pltpu.VMEM((nh, s, 2 * FOX_HEAD_DIM), BF16), pltpu.VMEM((nh, nq, acc_rows, tq), BF16)]
        + [pltpu.VMEM((tq, gw), F32)] * (2 * n_units)
        + [pltpu.VMEM((tq, gw), BF16)] * (2 * n_units)
        + [pltpu.VMEM((1, gw), F32)] * (2 * n_units)
        + [pltpu.VMEM((acc_rows, gw), F32)] * n_units,
        compiler_params=pltpu.CompilerParams(
            dimension_semantics=("parallel", "parallel", "arbitrary"), vmem_limit_bytes=VMEM_LIMIT_BYTES),
    )(qkvz, qkvz, qkvz, c2, qkvz)


def _stack_heads(x, head0):
    zero = jnp.zeros_like(x)
    return jnp.concatenate([jnp.where(head0, x, zero), jnp.where(head0, zero, x)], axis=0)


def _rwkv_chunks(units, masks):
    head0, strict, diag, blockdiag, eye = masks
    incl = strict | diag
    c = units[0]["r"].shape[0]
    n_units = range(len(units))
    zero = jnp.zeros((c, LANES), F32)
    nt = (((1,), (1,)), ((), ()))
    tn = (((0,), (0,)), ((), ()))

    rt, at, bh, kh, gc, sv, lhs1, rhs1 = [], [], [], [], [], [], [], []
    for u in units:
        lg = u["lg"]
        lgc = lg[c - 1:c, :]
        e_neg = jnp.exp2(-lg)
        e_end = jnp.exp2(lgc - lg)
        rt.append(u["r"] * jnp.exp2(lg))
        at.append(u["av"] * jnp.exp2(lg - u["logw"]))
        bh.append(u["bv"] * e_end)
        kh.append(u["k2"] * e_end)
        gc.append(jnp.exp2(lgc))
        sv.append(_stack_heads(u["v"], head0))
        lhs1.append(jnp.concatenate([at[-1], rt[-1]], axis=0))
        rhs1.append(jnp.concatenate([_stack_heads(u["bv"] * e_neg, head0),
                                     _stack_heads(u["k2"] * e_neg, head0)], axis=0))

    big = [_bdot(lhs1[i], rhs1[i], nt) for i in n_units]
    l_mat = [jnp.where(strict, big[i][:c, :LANES], zero) for i in n_units]
    a_ak = [jnp.where(strict, big[i][:c, LANES:], zero) for i in n_units]
    a_r = [jnp.concatenate([jnp.where(incl, big[i][c:, :LANES], zero),
                            jnp.where(incl, big[i][c:, LANES:], zero)], axis=1) for i in n_units]

    aakv = [_bdot(a_ak[i], sv[i]) for i in n_units]
    n_steps = max(1, (c - 1).bit_length())
    t_inv = [jnp.where(diag, 1.0, 0.0) + l_mat[i] for i in n_units]
    cur = [_bdot(l_mat[i], _stack_heads(l_mat[i], head0)) for i in n_units] if n_steps > 1 else None
    for step in range(1, n_steps):
        if step + 1 < n_steps:
            both = [_bdot(jnp.concatenate([cur[i], t_inv[i]], axis=0), _stack_heads(cur[i], head0)) for i in n_units]
            t_inv = [t_inv[i] + both[i][c:] for i in n_units]
            cur = [both[i][:c] for i in n_units]
        else:
            t_inv = [t_inv[i] + _bdot(t_inv[i], _stack_heads(cur[i], head0)) for i in n_units]
    x = [_bdot(t_inv[i], jnp.concatenate([_stack_heads(at[i], head0), _stack_heads(aakv[i], head0)], axis=1))
         for i in n_units]

    rhs2 = [jnp.concatenate(
        [jnp.concatenate([_stack_heads(x[i][:, :LANES], head0), _stack_heads(x[i][:, LANES:], head0)], axis=1),
         jnp.concatenate([jnp.zeros((2 * c, LANES), F32), sv[i]], axis=1)], axis=0) for i in n_units]
    qy = [_bdot(a_r[i], rhs2[i]) for i in n_units]
    gh = [_bdot(jnp.concatenate([bh[i], kh[i]], axis=0),
                jnp.concatenate([x[i], jnp.concatenate([zero, units[i]["v"]], axis=1)], axis=0), tn)
          for i in n_units]
    out = []
    for i in n_units:
        out.append((rt[i] + qy[i][:, :LANES], qy[i][:, LANES:],
                    jnp.where(blockdiag, gh[i][:, :LANES], 0.0) + eye * gc[i],
                    jnp.where(blockdiag, gh[i][:, LANES:], 0.0)))
    return out


def _rwkv_kernel(r_ref, k_ref, v_ref, lo_ref, z_ref, wup_ref, aup_ref, w0_ref, a0_ref, kk_ref, ka_ref, rk_ref,
                 gg_ref, gb_ref, o_ref, sbd_ref):
    nb, t, w = r_ref.shape
    c = RWKV_CHUNK
    n = RWKV_HEAD_DIM
    n_pairs = w // LANES
    n_chunks = t // c

    @pl.when(pl.program_id(1) == 0)
    def _():
        sbd_ref[...] = jnp.zeros_like(sbd_ref)

    wr = lax.broadcasted_iota(jnp.int32, (w, w), 0)
    wc = lax.broadcasted_iota(jnp.int32, (w, w), 1)
    head_ones = ((wr // n) == (wc // n)).astype(BF16)
    gr = min(t, 2 * c)
    tr = lax.broadcasted_iota(jnp.int32, (gr, gr), 0)
    tc = lax.broadcasted_iota(jnp.int32, (gr, gr), 1)
    chunk_tri = ((tc <= tr) & ((tc // c) == (tr // c))).astype(BF16)

    def chunk_cumsum(x):
        return jnp.concatenate([_split2_dot(chunk_tri, x[g0:g0 + gr]) for g0 in range(0, t, gr)], axis=0)

    def head_sum(x):
        return _bdot(x, head_ones)

    prepped = []
    for bi in range(nb):
        r, k, v, lora = r_ref[bi], k_ref[bi], v_ref[bi], lo_ref[bi]
        dec = _bdot(jnp.tanh(lora), wup_ref[...])
        logw = -(EXP_NEG_HALF * LOG2_E) * _sigmoid(w0_ref[...] + dec)
        a = _sigmoid(a0_ref[...] + _bdot(lora, aup_ref[...]))
        kk = k * kk_ref[...]
        kk = kk * lax.rsqrt(jnp.maximum(head_sum(kk * kk), 1e-24))
        k2 = k * (1.0 + (a - 1.0) * ka_ref[...])
        prepped.append(dict(r=r, k2=k2, v=v, av=-kk, bv=kk * a, logw=logw, lg=chunk_cumsum(logw)))

    lane = lax.broadcasted_iota(jnp.int32, (c, LANES), 1)
    row = lax.broadcasted_iota(jnp.int32, (c, LANES), 0)
    head0 = lane < n
    col = jnp.where(head0, lane, lane - n)
    lane_r = lax.broadcasted_iota(jnp.int32, (LANES, LANES), 0)
    lane_c = lax.broadcasted_iota(jnp.int32, (LANES, LANES), 1)
    masks = (head0, col < row, col == row, (lane_r < n) == (lane_c < n), (lane_r == lane_c).astype(F32))

    seqs = [(bi, pi) for bi in range(nb) for pi in range(n_pairs)]
    units = [{name: arr[ci * c:(ci + 1) * c, pi * LANES:(pi + 1) * LANES] for name, arr in prepped[bi].items()}
             for ci in range(n_chunks) for (bi, pi) in seqs]
    affine = _rwkv_chunks(units, masks)

    sbd = [sbd_ref[si] for si in range(len(seqs))]
    ys = {}
    for ci in range(n_chunks):
        qs = [affine[ci * len(seqs) + si] for si in range(len(seqs))]
        both = [_bdot(jnp.concatenate([qs[si][0], qs[si][2]], axis=0), sbd[si]) for si in range(len(seqs))]
        sbd = [both[si][c:] + qs[si][3] for si in range(len(seqs))]
        for si, seq in enumerate(seqs):
            ys[seq + (ci,)] = both[si][:c] + qs[si][1]
    for si in range(len(seqs)):
        sbd_ref[si] = sbd[si]

    for bi in range(nb):
        y = jnp.concatenate([jnp.concatenate([ys[(bi, pi, ci)] for pi in range(n_pairs)], axis=1)
                             for ci in range(n_chunks)], axis=0)
        p = prepped[bi]
        mu = head_sum(y) * (1.0 / n)
        d = y - mu
        var = head_sum(d * d) * (1.0 / n)
        y = d * lax.rsqrt(var + GN_EPS) * gg_ref[...] + gb_ref[...]
        y = y + head_sum(p["r"] * p["k2"] * rk_ref[...]) * p["v"]
        o_ref[bi] = (y * z_ref[bi].astype(F32)).astype(o_ref.dtype)


def _rwkv_mix(rkvl, qkvz, rows, wup_pad, aup_pad, n_pairs, lora_cols, z_col0, t, pairs_per_step):
    b, s, _ = rkvl.shape
    t = min(t, s)
    w = pairs_per_step * LANES
    p = n_pairs // pairs_per_step
    assert p * pairs_per_step == n_pairs and z_col0 % w == 0
    lora_block = 3 * n_pairs * LANES // lora_cols
    assert lora_block * lora_cols == 3 * n_pairs * LANES
    tok = lambda off: pl.BlockSpec((b, t, w), lambda j, i: (0, i, off + j))
    prow = lambda off: pl.BlockSpec((1, w), lambda j, i: (0, off + j))
    in_specs = [
        tok(0), tok(p), tok(2 * p),
        pl.BlockSpec((b, t, lora_cols), lambda j, i: (0, i, lora_block)),
        tok(z_col0 // w),
        pl.BlockSpec((lora_cols, w), lambda j, i: (0, j)),
        pl.BlockSpec((lora_cols, w), lambda j, i: (0, j)),
    ] + [prow(0)] * 7
    return pl.pallas_call(
        _rwkv_kernel,
        grid=(p, s // t),
        in_specs=in_specs,
        out_specs=pl.BlockSpec((b, t, w), lambda j, i: (0, i, j)),
        out_shape=jax.ShapeDtypeStruct((b, s, n_pairs * LANES), BF16),
        name="rwkv7_mix",
        scratch_shapes=[pltpu.VMEM((b * pairs_per_step, LANES, LANES), F32)],
        compiler_params=pltpu.CompilerParams(
            dimension_semantics=("parallel", "arbitrary"), vmem_limit_bytes=VMEM_LIMIT_BYTES),
    )(rkvl, rkvl, rkvl, rkvl, qkvz, wup_pad, aup_pad,
      rows["w0"], rows["a0"], rows["k_k"], rows["k_a"], rows["r_k"], rows["gn_gain"], rows["gn_bias"])


def _out_kernel(hf_ref, hr_ref, wt_ref, wb_ref, x_ref, g_ref, b_ref, o_ref, sum_ref, *, alpha, n_tiles):
    j = pl.program_id(1)
    tn = x_ref.shape[1]
    out = (jnp.dot(hf_ref[...], wt_ref[...], preferred_element_type=F32)
           + jnp.dot(hr_ref[...], wb_ref[...], preferred_element_type=F32))
    pre = alpha * x_ref[...] + out
    o_ref[:, pl.ds(pl.multiple_of(j * tn, tn), tn)] = pre
    part = jnp.sum(pre, axis=-1, keepdims=True)
    sum_ref[...] = jnp.where(j == 0, part, sum_ref[...] + part)

    @pl.when(j == n_tiles - 1)
    def _():
        d = n_tiles * tn
        tiles = [slice(jj * tn, (jj + 1) * tn) for jj in range(n_tiles)]
        mean = sum_ref[...] * (1.0 / d)
        sq = jnp.sum(jnp.square(o_ref[:, tiles[0]] - mean), axis=-1, keepdims=True)
        for sl in tiles[1:]:
            sq = sq + jnp.sum(jnp.square(o_ref[:, sl] - mean), axis=-1, keepdims=True)
        inv = lax.rsqrt(sq * (1.0 / d) + LN_EPS)
        for sl in tiles:
            o_ref[:, sl] = (o_ref[:, sl] - mean) * inv * g_ref[:, sl] + b_ref[:, sl]


def _out_proj_layernorm(hf, hr, w_out_bf16, x2, ln_gain, ln_bias, alpha, tm, tn):
    m, d = x2.shape
    kf, kr = hf.shape[1], hr.shape[1]
    assert kf == kr and kf + kr == w_out_bf16.shape[0]
    assert x2.dtype == F32, "the output block doubles as the f32 pre-norm accumulator"
    tm, tn = min(tm, m), min(tn, d)
    n_tiles = d // tn
    return pl.pallas_call(
        functools.partial(_out_kernel, alpha=alpha, n_tiles=n_tiles),
        grid=(m // tm, n_tiles),
        in_specs=[pl.BlockSpec((tm, kf), lambda i, j: (i, 0)),
                  pl.BlockSpec((tm, kr), lambda i, j: (i, 0)),
                  pl.BlockSpec((kf, tn), lambda i, j: (0, j)),
                  pl.BlockSpec((kr, tn), lambda i, j: (1, j)),
                  pl.BlockSpec((tm, tn), lambda i, j: (i, j)),
                  pl.BlockSpec((1, d), lambda i, j: (0, 0)),
                  pl.BlockSpec((1, d), lambda i, j: (0, 0))],
        out_specs=pl.BlockSpec((tm, d), lambda i, j: (i, 0)),
        out_shape=jax.ShapeDtypeStruct((m, d), x2.dtype),
        name="out_proj_layernorm",
        scratch_shapes=[pltpu.VMEM((tm, 1), F32)],
        compiler_params=pltpu.CompilerParams(
            dimension_semantics=("parallel", "arbitrary"), vmem_limit_bytes=VMEM_LIMIT_BYTES),
    )(hf, hr, w_out_bf16, w_out_bf16, x2, ln_gain.reshape(1, d).astype(F32), ln_bias.reshape(1, d).astype(F32))


def kernel(x, w_in, f_bias, mu_shift, w0, w_up, a0, a_up, k_k, k_a, r_k, gn_gain, gn_bias, w_out, ln_gain, ln_bias):
    b, s, d = x.shape
    fox_heads = f_bias.shape[0]
    fw = fox_heads * FOX_HEAD_DIM
    rw = w0.shape[0]
    n_pairs = rw // LANES
    assert r_k.shape[1] == RWKV_HEAD_DIM and rw % LANES == 0
    dl, al = w_up.shape[0], a_up.shape[0]
    mix = fw + rw
    assert w_in.shape[1] == 3 * fw + fox_heads + 3 * rw + dl + al + mix
    depth = 1
    alpha = (2 * depth) ** 0.25

    o_ff = 3 * fw
    o_r = o_ff + fox_heads
    o_wd = o_r + 3 * rw
    o_ad = o_wd + dl
    o_z = o_ad + al

    lora_cols = -(-(dl + al) // 256) * 256
    assert (3 * rw) % lora_cols == 0 and fox_heads <= LANES
    assert o_r + 3 * rw + lora_cols <= w_in.shape[1] and o_ff + LANES <= w_in.shape[1]
    assert all(off % BF16_SUBLANES == 0 for off in (o_ff, o_r, o_z)), "weight row windows must be tile aligned"

    x2 = x.reshape(b * s, d)
    w_t = w_in.T.astype(BF16)
    tn_bf = _pick_tile(math.gcd(fw, mix), 1024)
    tn_f32 = _pick_tile(3 * rw + lora_cols, 1280)
    mu_row = jnp.concatenate([mu_shift.astype(F32), jnp.zeros((lora_cols - dl - al,), F32)]).reshape(1, -1)

    xb = x2.astype(BF16)
    qkvz = _in_proj_fox(xb, w_t, 1024, tn_bf, fw // tn_bf, 3 * fw // tn_bf, o_z, mix // tn_bf,
                        LOG2_E * FOX_HEAD_DIM ** -0.5).reshape(b, s, 3 * fw + mix)
    rkvl, ff = _in_proj_rwkv(xb, w_t, mu_row, 512, tn_f32, o_r, (3 * rw + lora_cols) // tn_f32, o_ff, s)
    rkvl = rkvl.reshape(b, s, 3 * rw + lora_cols)

    bias_row = jnp.zeros((1, LANES), F32).at[0, :fox_heads].set(f_bias.astype(F32))
    c2 = _gate_cumsum(ff.reshape(b, s, LANES), bias_row, 0, 512)
    hf = _fox_attention(qkvz, c2, fox_heads, min(512, s), 0, 2)

    pad_rows = lambda w, off: jnp.zeros((lora_cols, rw), F32).at[off:off + w.shape[0]].set(w.astype(F32)).astype(BF16)
    rows = {}
    for name, val in (("w0", w0), ("a0", a0), ("k_k", k_k), ("k_a", k_a), ("r_k", r_k),
                      ("gn_gain", gn_gain), ("gn_bias", gn_bias)):
        rows[name] = val.astype(F32).reshape(1, rw)
    hr = _rwkv_mix(rkvl, qkvz, rows, pad_rows(w_up, 0), pad_rows(a_up, dl), n_pairs, lora_cols,
                   3 * fw + fw, 512, 2)

    out = _out_proj_layernorm(hf.reshape(b * s, fw), hr.reshape(b * s, rw), w_out.astype(BF16),
                              x2, ln_gain, ln_bias, alpha, 512, 1024)
    return out.reshape(b, s, d)
```

```python
import functools
import math

import jax
import jax.numpy as jnp
from jax import lax
from jax.experimental import pallas as pl
from jax.experimental.pallas import tpu as pltpu

F32 = jnp.float32
BF16 = jnp.bfloat16

LANES = 128
BF16_SUBLANES = 16
FOX_HEAD_DIM = 128
RWKV_HEAD_DIM = 64
RWKV_CHUNK = 64
LOG2_E = 1.4426950408889634
EXP_NEG_HALF = 0.6065306597126334
LN_EPS = 1e-5
GN_EPS = 64e-5
VMEM_LIMIT_BYTES = 56 * 1024 * 1024


def _bdot(a, b, dims=(((1,), (0,)), ((), ()))):
    return lax.dot_general(a.astype(BF16), b.astype(BF16), dims, preferred_element_type=F32)


def _sigmoid(x):
    return 0.5 * jnp.tanh(0.5 * x) + 0.5


def _split3_dot(m_bf16, x):
    hi = x.astype(BF16)
    r1 = x - hi.astype(F32)
    mid = r1.astype(BF16)
    lo = (r1 - mid.astype(F32)).astype(BF16)
    dot = functools.partial(jnp.dot, preferred_element_type=F32)
    return dot(m_bf16, hi) + dot(m_bf16, mid) + dot(m_bf16, lo)


def _split2_dot(m_bf16, x):
    hi = x.astype(BF16)
    lo = (x - hi.astype(F32)).astype(BF16)
    dot = functools.partial(jnp.dot, preferred_element_type=F32)
    return dot(m_bf16, hi) + dot(m_bf16, lo)


def _pick_tile(n, preferred):
    t = min(preferred, n) // LANES * LANES
    while n % t:
        t -= LANES
    return t


_NT = (((1,), (1,)), ((), ()))


def _proj_fox_kernel(a_ref, wt_ref, o_ref, *, n_q_blocks, n_qkv_blocks, q_scale):
    j = pl.program_id(1)
    acc = lax.dot_general(a_ref[...], wt_ref[...], _NT, preferred_element_type=F32)
    factor = jnp.where(j >= n_qkv_blocks, _sigmoid(acc), jnp.where(j < n_q_blocks, q_scale, 1.0))
    o_ref[...] = (acc * factor).astype(o_ref.dtype)


def _proj_rwkv_kernel(a_ref, wt_ref, et_ref, mu_ref, o_ref, e_ref, last_ref, *, blocks_per_seq):
    i, j = pl.program_id(0), pl.program_id(1)

    @pl.when(j == 0)
    def _():
        e_ref[...] = lax.dot_general(a_ref[...], et_ref[...], _NT, preferred_element_type=F32)

    p = lax.dot_general(a_ref[...], wt_ref[...], _NT, preferred_element_type=F32)
    tm = p.shape[0]
    above = jnp.where(i % blocks_per_seq == 0, 0.0, last_ref[j])
    first_row = lax.broadcasted_iota(jnp.int32, (tm, 1), 0) == 0
    prev = jnp.where(first_row, above, pltpu.roll(p, 1, axis=0))
    last_ref[j] = p[tm - 1:tm, :]
    o_ref[...] = p + (prev - p) * mu_ref[...]


def _w_row_map(segments, tn):
    def w_row(j):
        row, first = None, 0
        for nb, row0 in segments:
            here = row0 + (j - first) * tn
            row = here if row is None else jnp.where(j >= first, here, row)
            first += nb
        return pl.multiple_of(row, BF16_SUBLANES)
    return w_row


def _in_proj_fox(a, w_t, tm, tn, n_q_blocks, n_qkv_blocks, z_row0, n_z_blocks, q_scale):
    m, k = a.shape
    tm = _pick_tile(m, tm)
    w_row = _w_row_map([(n_qkv_blocks, 0), (n_z_blocks, z_row0)], tn)
    n_blocks = n_qkv_blocks + n_z_blocks
    return pl.pallas_call(
        functools.partial(_proj_fox_kernel, n_q_blocks=n_q_blocks, n_qkv_blocks=n_qkv_blocks, q_scale=q_scale),
        grid=(m // tm, n_blocks),
        in_specs=[pl.BlockSpec((tm, k), lambda i, j: (i, 0)),
                  pl.BlockSpec((pl.Element(tn), pl.Element(k)), lambda i, j: (w_row(j), 0))],
        out_specs=pl.BlockSpec((tm, tn), lambda i, j: (i, j)),
        out_shape=jax.ShapeDtypeStruct((m, n_blocks * tn), BF16),
        name="in_proj_fox",
        compiler_params=pltpu.CompilerParams(
            dimension_semantics=("parallel", "arbitrary"), vmem_limit_bytes=VMEM_LIMIT_BYTES),
    )(a, w_t)


def _in_proj_rwkv(a, w_t, mu_row, tm, tn, row0, n_blocks, gate_row0, seq_len):
    m, k = a.shape
    tm = _pick_tile(math.gcd(m, seq_len), tm)
    w_row = _w_row_map([(n_blocks, row0)], tn)
    return pl.pallas_call(
        functools.partial(_proj_rwkv_kernel, blocks_per_seq=seq_len // tm),
        grid=(m // tm, n_blocks),
        in_specs=[pl.BlockSpec((tm, k), lambda i, j: (i, 0)),
                  pl.BlockSpec((pl.Element(tn), pl.Element(k)), lambda i, j: (w_row(j), 0)),
                  pl.BlockSpec((pl.Element(LANES), pl.Element(k)), lambda i, j: (gate_row0, 0)),
                  pl.BlockSpec((1, tn), lambda i, j: (0, j))],
        out_specs=[pl.BlockSpec((tm, tn), lambda i, j: (i, j)),
                   pl.BlockSpec((tm, LANES), lambda i, j: (i, 0))],
        out_shape=[jax.ShapeDtypeStruct((m, n_blocks * tn), F32), jax.ShapeDtypeStruct((m, LANES), F32)],
        name="in_proj_rwkv",
        scratch_shapes=[pltpu.VMEM((n_blocks, 1, tn), F32)],
        compiler_params=pltpu.CompilerParams(
            dimension_semantics=("arbitrary", "arbitrary"), vmem_limit_bytes=VMEM_LIMIT_BYTES),
    )(a, w_t, w_t, mu_row)


def _gate_kernel(x_ref, b_ref, o_ref, carry_ref):
    @pl.when(pl.program_id(1) == 0)
    def _():
        carry_ref[...] = jnp.zeros_like(carry_ref)

    t = x_ref.shape[1]
    x = x_ref[0] + b_ref[...]
    logf = jnp.minimum(x, 0.0) - jnp.log1p(jnp.exp(-jnp.abs(x)))
    tri = (lax.broadcasted_iota(jnp.int32, (t, t), 1)
           <= lax.broadcasted_iota(jnp.int32, (t, t), 0)).astype(BF16)
    c = _split3_dot(tri, logf) + carry_ref[0:1, :]
    o_ref[0] = c * LOG2_E
    carry_ref[0:1, :] = c[t - 1:t, :]


def _gate_cumsum(lora3, bias_row, col_block, t):
    b, s, _ = lora3.shape
    t = min(t, s)
    return pl.pallas_call(
        _gate_kernel,
        grid=(b, s // t),
        in_specs=[pl.BlockSpec((1, t, LANES), lambda bi, i: (bi, i, col_block)),
                  pl.BlockSpec((1, LANES), lambda bi, i: (0, 0))],
        out_specs=pl.BlockSpec((1, t, LANES), lambda bi, i: (bi, i, 0)),
        out_shape=jax.ShapeDtypeStruct((b, s, LANES), F32),
        name="fox_gate_cumsum",
        scratch_shapes=[pltpu.VMEM((8, LANES), F32)],
        compiler_params=pltpu.CompilerParams(dimension_semantics=("parallel", "arbitrary")),
    )(lora3, bias_row)


def _fox_kernel(q_ref, k_ref, v_ref, c_ref, z_ref, o_ref, kaug_ref, vt_ref, *unit_refs, tq, c_lane0, group_cols):
    i = pl.program_id(2)
    s_len = k_ref.shape[1]
    dh = FOX_HEAD_DIM
    n_heads = k_ref.shape[2] // dh
    gw = group_cols
    n_groups = tq // gw
    units = [(hh, g) for hh in range(n_heads) for g in range(n_groups)]
    nu = len(units)
    sa, sb, pa, pb, ta, tb, m_refs, al_refs, acc_refs = (unit_refs[kind * nu:(kind + 1) * nu] for kind in range(9))

    @pl.when(i == 0)
    def _():
        lane_r = lax.broadcasted_iota(jnp.int32, (LANES, LANES), 0)
        lane_c = lax.broadcasted_iota(jnp.int32, (LANES, LANES), 1)
        dot = functools.partial(jnp.dot, preferred_element_type=F32)
        for hh in range(n_heads):
            c_lane = c_lane0 + pl.program_id(1) * n_heads + hh
            pick = [jnp.where((lane_r == c_lane) & (lane_c == term), 1.0, 0.0).astype(BF16) for term in range(3)]
            hs = slice(hh * dh, (hh + 1) * dh)
            for jb in range(s_len // tq):
                sl = slice(jb * tq, (jb + 1) * tq)
                c = c_ref[0, sl, :]
                hi = c.astype(BF16)
                r1 = c - hi.astype(F32)
                mid = r1.astype(BF16)
                lo = (r1 - mid.astype(F32)).astype(BF16)
                c_terms = dot(hi, pick[0]) + dot(mid, pick[1]) + dot(lo, pick[2])
                kaug_ref[hh, sl, :dh] = k_ref[0, sl, hs]
                kaug_ref[hh, sl, dh:] = (-c_terms).astype(BF16)
                vt_ref[hh, jb, :dh, :] = v_ref[0, sl, hs].astype(F32).T.astype(BF16)
                vt_ref[hh, jb, dh:, :] = jnp.ones((vt_ref.shape[2] - dh, tq), BF16)

    ones3 = jnp.where(lax.broadcasted_iota(jnp.int32, (dh, tq), 0) < 3, 1.0, 0.0)
    q_t = [jnp.concatenate([q_ref[0, :, hh * dh:(hh + 1) * dh].astype(F32).T, ones3], axis=0).astype(BF16)
           for hh in range(n_heads)]
    q_u = [q_t[hh][:, g * gw:(g + 1) * gw] for (hh, g) in units]

    def scores(j, s_out, top_out):
        rows = pl.ds(pl.multiple_of(j * tq, tq), tq)
        for u, (hh, _) in enumerate(units):
            s = jnp.dot(kaug_ref[hh, rows, :], q_u[u], preferred_element_type=F32)
            s_out[u][...] = s
            top_out[u][...] = jnp.max(s, axis=0, keepdims=True)

    def accumulate(p, j):
        for u, (hh, _) in enumerate(units):
            acc_refs[u][...] = (al_refs[u][...] * acc_refs[u][...]
                                + jnp.dot(vt_ref[hh, j], p[u], preferred_element_type=F32))

    def softmax_step(s, top):
        p = []
        for u in range(nu):
            m = m_refs[u][...]
            m_new = jnp.maximum(m, top[u])
            p.append(jnp.exp2(s[u] - m_new).astype(BF16))
            al_refs[u][...] = jnp.exp2(m - m_new)
            m_refs[u][...] = m_new
        return p

    def step(j, s_in, s_out, top_in, top_out, p_in, p_out):
        scores(j + 1, s_out, top_out)
        accumulate([p_in[u][...] for u in range(nu)], jnp.maximum(j - 1, 0))
        p = softmax_step([s_in[u][...] for u in range(nu)], [top_in[u][...] for u in range(nu)])
        for u in range(nu):
            p_out[u][...] = p[u]

    scores(0, sa, ta)
    for u in range(nu):
        pa[u][...] = jnp.zeros_like(pa[u])
        m_refs[u][...] = jnp.full(m_refs[u].shape, -jnp.inf, F32)
        al_refs[u][...] = jnp.ones_like(al_refs[u])
        acc_refs[u][...] = jnp.zeros_like(acc_refs[u])

    def pair(t, carry):
        step(2 * t, sa, sb, ta, tb, pa, pb)
        step(2 * t + 1, sb, sa, tb, ta, pb, pa)
        return carry

    lax.fori_loop(0, i // 2, pair, 0)

    def finish(s_last, p_last):
        accumulate([p_last[u][...] for u in range(nu)], jnp.maximum(i - 1, 0))
        key = lax.broadcasted_iota(jnp.int32, (tq, gw), 0)
        qry = lax.broadcasted_iota(jnp.int32, (tq, gw), 1)
        s_diag = [jnp.where(key <= qry + g * gw, s_last[u][...], -jnp.inf) for u, (_, g) in enumerate(units)]
        p = softmax_step(s_diag, [jnp.max(s_u, axis=0, keepdims=True) for s_u in s_diag])
        accumulate(p, i)
        for u, (hh, g) in enumerate(units):
            acc_u = acc_refs[u][...]
            gate = z_ref[0, g * gw:(g + 1) * gw, hh * dh:(hh + 1) * dh].astype(F32)
            out = (acc_u[:dh, :] / acc_u[dh:dh + 1, :]).T
            o_ref[0, g * gw:(g + 1) * gw, hh * dh:(hh + 1) * dh] = (out * gate).astype(o_ref.dtype)

    @pl.when(i % 2 == 1)
    def _():
        step(i - 1, sa, sb, ta, tb, pa, pb)
        finish(sb, pb)

    @pl.when(i % 2 == 0)
    def _():
        finish(sa, pa)


def _fox_attention(qkvz, c2, n_heads, tq, c_lane0, heads_per_step):
    b, s, _ = qkvz.shape
    nq = s // tq
    nh = heads_per_step
    hb = n_heads // nh
    assert hb * nh == n_heads
    w = nh * FOX_HEAD_DIM
    acc_rows = FOX_HEAD_DIM + BF16_SUBLANES
    gw = min(256, tq)
    n_units = nh * (tq // gw)
    return pl.pallas_call(
        functools.partial(_fox_kernel, tq=tq, c_lane0=c_lane0, group_cols=gw),
        grid=(b, hb, nq),
        in_specs=[pl.BlockSpec((1, tq, w), lambda bi, hi, i: (bi, i, hi)),
                  pl.BlockSpec((1, s, w), lambda bi, hi, i: (bi, 0, hb + hi)),
                  pl.BlockSpec((1, s, w), lambda bi, hi, i: (bi, 0, 2 * hb + hi)),
                  pl.BlockSpec((1, s, LANES), lambda bi, hi, i: (bi, 0, 0)),
                  pl.BlockSpec((1, tq, w), lambda bi, hi, i: (bi, i, 3 * hb + hi))],
        out_specs=pl.BlockSpec((1, tq, w), lambda bi, hi, i: (bi, i, hi)),
        out_shape=jax.ShapeDtypeStruct((b, s, n_heads * FOX_HEAD_DIM), BF16),
        name="fox_attention",
        scratch_shapes=[pltpu.VMEM((nh, s, 2 * FOX_HEAD_DIM), BF16), pltpu.VMEM((nh, nq, acc_rows, tq), BF16)]
        + [pltpu.VMEM((tq, gw), F32)] * (2 * n_units)
        + [pltpu.VMEM((tq, gw), BF16)] * (2 * n_units)
        + [pltpu.VMEM((1, gw), F32)] * (4 * n_units)
        + [pltpu.VMEM((acc_rows, gw), F32)] * n_units,
        compiler_params=pltpu.CompilerParams(
            dimension_semantics=("parallel", "parallel", "arbitrary"), vmem_limit_bytes=VMEM_LIMIT_BYTES),
    )(qkvz, qkvz, qkvz, c2, qkvz)


def _stack_heads(x, head0):
    zero = jnp.zeros_like(x)
    return jnp.concatenate([jnp.where(head0, x, zero), jnp.where(head0, zero, x)], axis=0)


def _rwkv_chunks(units, masks):
    head0, strict, diag, blockdiag, eye = masks
    incl = strict | diag
    c = units[0]["r"].shape[0]
    n_units = range(len(units))
    zero = jnp.zeros((c, LANES), F32)
    nt = (((1,), (1,)), ((), ()))
    tn = (((0,), (0,)), ((), ()))

    rt, at, bh, kh, gc, sv, lhs1, rhs1 = [], [], [], [], [], [], [], []
    for u in units:
        lg = u["lg"]
        lgc = lg[c - 1:c, :]
        e_neg = jnp.exp2(-lg)
        e_end = jnp.exp2(lgc - lg)
        rt.append(u["r"] * jnp.exp2(lg))
        at.append(u["av"] * jnp.exp2(lg - u["logw"]))
        bh.append(u["bv"] * e_end)
        kh.append(u["k2"] * e_end)
        gc.append(jnp.exp2(lgc))
        sv.append(_stack_heads(u["v"], head0))
        lhs1.append(jnp.concatenate([at[-1], rt[-1]], axis=0))
        rhs1.append(jnp.concatenate([_stack_heads(u["bv"] * e_neg, head0),
                                     _stack_heads(u["k2"] * e_neg, head0)], axis=0))

    big = [_bdot(lhs1[i], rhs1[i], nt) for i in n_units]
    l_mat = [jnp.where(strict, big[i][:c, :LANES], zero) for i in n_units]
    a_ak = [jnp.where(strict, big[i][:c, LANES:], zero) for i in n_units]
    a_r = [jnp.concatenate([jnp.where(incl, big[i][c:, :LANES], zero),
                            jnp.where(incl, big[i][c:, LANES:], zero)], axis=1) for i in n_units]

    aakv = [_bdot(a_ak[i], sv[i]) for i in n_units]
    n_steps = max(1, (c - 1).bit_length())
    t_inv = [jnp.where(diag, 1.0, 0.0) + l_mat[i] for i in n_units]
    cur = [_bdot(l_mat[i], _stack_heads(l_mat[i], head0)) for i in n_units] if n_steps > 1 else None
    for step in range(1, n_steps):
        if step + 1 < n_steps:
            both = [_bdot(jnp.concatenate([cur[i], t_inv[i]], axis=0), _stack_heads(cur[i], head0)) for i in n_units]
            t_inv = [t_inv[i] + both[i][c:] for i in n_units]
            cur = [both[i][:c] for i in n_units]
        else:
            t_inv = [t_inv[i] + _bdot(t_inv[i], _stack_heads(cur[i], head0)) for i in n_units]
    x = [_bdot(t_inv[i], jnp.concatenate([_stack_heads(at[i], head0), _stack_heads(aakv[i], head0)], axis=1))
         for i in n_units]

    rhs2 = [jnp.concatenate(
        [jnp.concatenate([_stack_heads(x[i][:, :LANES], head0), _stack_heads(x[i][:, LANES:], head0)], axis=1),
         jnp.concatenate([jnp.zeros((2 * c, LANES), F32), sv[i]], axis=1)], axis=0) for i in n_units]
    qy = [_bdot(a_r[i], rhs2[i]) for i in n_units]
    gh = [_bdot(jnp.concatenate([bh[i], kh[i]], axis=0),
                jnp.concatenate([x[i], jnp.concatenate([zero, units[i]["v"]], axis=1)], axis=0), tn)
          for i in n_units]
    out = []
    for i in n_units:
        out.append((rt[i] + qy[i][:, :LANES], qy[i][:, LANES:],
                    jnp.where(blockdiag, gh[i][:, :LANES], 0.0) + eye * gc[i],
                    jnp.where(blockdiag, gh[i][:, LANES:], 0.0)))
    return out


def _rwkv_kernel(r_ref, k_ref, v_ref, lo_ref, z_ref, wup_ref, aup_ref, w0_ref, a0_ref, kk_ref, ka_ref, rk_ref,
                 gg_ref, gb_ref, o_ref, sbd_ref):
    nb, t, w = r_ref.shape
    c = RWKV_CHUNK
    n = RWKV_HEAD_DIM
    n_pairs = w // LANES
    n_chunks = t // c

    @pl.when(pl.program_id(1) == 0)
    def _():
        sbd_ref[...] = jnp.zeros_like(sbd_ref)

    wr = lax.broadcasted_iota(jnp.int32, (w, w), 0)
    wc = lax.broadcasted_iota(jnp.int32, (w, w), 1)
    head_ones = ((wr // n) == (wc // n)).astype(BF16)
    gr = min(t, 2 * c)
    tr = lax.broadcasted_iota(jnp.int32, (gr, gr), 0)
    tc = lax.broadcasted_iota(jnp.int32, (gr, gr), 1)
    chunk_tri = ((tc <= tr) & ((tc // c) == (tr // c))).astype(BF16)

    def chunk_cumsum(x):
        return jnp.concatenate([_split2_dot(chunk_tri, x[g0:g0 + gr]) for g0 in range(0, t, gr)], axis=0)

    def head_sum(x):
        return _bdot(x, head_ones)

    prepped = []
    for bi in range(nb):
        r, k, v, lora = r_ref[bi], k_ref[bi], v_ref[bi], lo_ref[bi]
        dec = _bdot(jnp.tanh(lora), wup_ref[...])
        logw = -(EXP_NEG_HALF * LOG2_E) * _sigmoid(w0_ref[...] + dec)
        a = _sigmoid(a0_ref[...] + _bdot(lora, aup_ref[...]))
        kk = k * kk_ref[...]
        kk = kk * lax.rsqrt(jnp.maximum(head_sum(kk * kk), 1e-24))
        k2 = k * (1.0 + (a - 1.0) * ka_ref[...])
        prepped.append(dict(r=r, k2=k2, v=v, av=-kk, bv=kk * a, logw=logw, lg=chunk_cumsum(logw)))

    lane = lax.broadcasted_iota(jnp.int32, (c, LANES), 1)
    row = lax.broadcasted_iota(jnp.int32, (c, LANES), 0)
    head0 = lane < n
    col = jnp.where(head0, lane, lane - n)
    lane_r = lax.broadcasted_iota(jnp.int32, (LANES, LANES), 0)
    lane_c = lax.broadcasted_iota(jnp.int32, (LANES, LANES), 1)
    masks = (head0, col < row, col == row, (lane_r < n) == (lane_c < n), (lane_r == lane_c).astype(F32))

    seqs = [(bi, pi) for bi in range(nb) for pi in range(n_pairs)]
    units = [{name: arr[ci * c:(ci + 1) * c, pi * LANES:(pi + 1) * LANES] for name, arr in prepped[bi].items()}
             for ci in range(n_chunks) for (bi, pi) in seqs]
    affine = _rwkv_chunks(units, masks)

    sbd = [sbd_ref[si] for si in range(len(seqs))]
    ys = {}
    for ci in range(n_chunks):
        qs = [affine[ci * len(seqs) + si] for si in range(len(seqs))]
        both = [_bdot(jnp.concatenate([qs[si][0], qs[si][2]], axis=0), sbd[si]) for si in range(len(seqs))]
        sbd = [both[si][c:] + qs[si][3] for si in range(len(seqs))]
        for si, seq in enumerate(seqs):
            ys[seq + (ci,)] = both[si][:c] + qs[si][1]
    for si in range(len(seqs)):
        sbd_ref[si] = sbd[si]

    for bi in range(nb):
        y = jnp.concatenate([jnp.concatenate([ys[(bi, pi, ci)] for pi in range(n_pairs)], axis=1)
                             for ci in range(n_chunks)], axis=0)
        p = prepped[bi]
        mu = head_sum(y) * (1.0 / n)
        d = y - mu
        var = head_sum(d * d) * (1.0 / n)
        y = d * lax.rsqrt(var + GN_EPS) * gg_ref[...] + gb_ref[...]
        y = y + head_sum(p["r"] * p["k2"] * rk_ref[...]) * p["v"]
        o_ref[bi] = (y * z_ref[bi].astype(F32)).astype(o_ref.dtype)


def _rwkv_mix(rkvl, qkvz, rows, wup_pad, aup_pad, n_pairs, lora_cols, z_col0, t, pairs_per_step):
    b, s, _ = rkvl.shape
    t = min(t, s)
    w = pairs_per_step * LANES
    p = n_pairs // pairs_per_step
    assert p * pairs_per_step == n_pairs and z_col0 % w == 0
    lora_block = 3 * n_pairs * LANES // lora_cols
    assert lora_block * lora_cols == 3 * n_pairs * LANES
    tok = lambda off: pl.BlockSpec((b, t, w), lambda j, i: (0, i, off + j))
    prow = lambda off: pl.BlockSpec((1, w), lambda j, i: (0, off + j))
    in_specs = [
        tok(0), tok(p), tok(2 * p),
        pl.BlockSpec((b, t, lora_cols), lambda j, i: (0, i, lora_block)),
        tok(z_col0 // w),
        pl.BlockSpec((lora_cols, w), lambda j, i: (0, j)),
        pl.BlockSpec((lora_cols, w), lambda j, i: (0, j)),
    ] + [prow(0)] * 7
    return pl.pallas_call(
        _rwkv_kernel,
        grid=(p, s // t),
        in_specs=in_specs,
        out_specs=pl.BlockSpec((b, t, w), lambda j, i: (0, i, j)),
        out_shape=jax.ShapeDtypeStruct((b, s, n_pairs * LANES), BF16),
        name="rwkv7_mix",
        scratch_shapes=[pltpu.VMEM((b * pairs_per_step, LANES, LANES), F32)],
        compiler_params=pltpu.CompilerParams(
            dimension_semantics=("parallel", "arbitrary"), vmem_limit_bytes=VMEM_LIMIT_BYTES),
    )(rkvl, rkvl, rkvl, rkvl, qkvz, wup_pad, aup_pad,
      rows["w0"], rows["a0"], rows["k_k"], rows["k_a"], rows["r_k"], rows["gn_gain"], rows["gn_bias"])


def _out_kernel(hf_ref, hr_ref, wt_ref, wb_ref, x_ref, g_ref, b_ref, o_ref, sum_ref, *, alpha, n_tiles):
    j = pl.program_id(1)
    tn = x_ref.shape[1]
    out = (jnp.dot(hf_ref[...], wt_ref[...], preferred_element_type=F32)
           + jnp.dot(hr_ref[...], wb_ref[...], preferred_element_type=F32))
    pre = alpha * x_ref[...] + out
    o_ref[:, pl.ds(pl.multiple_of(j * tn, tn), tn)] = pre
    part = jnp.sum(pre, axis=-1, keepdims=True)
    sum_ref[...] = jnp.where(j == 0, part, sum_ref[...] + part)

    @pl.when(j == n_tiles - 1)
    def _():
        d = n_tiles * tn
        tiles = [slice(jj * tn, (jj + 1) * tn) for jj in range(n_tiles)]
        mean = sum_ref[...] * (1.0 / d)
        sq = jnp.sum(jnp.square(o_ref[:, tiles[0]] - mean), axis=-1, keepdims=True)
        for sl in tiles[1:]:
            sq = sq + jnp.sum(jnp.square(o_ref[:, sl] - mean), axis=-1, keepdims=True)
        inv = lax.rsqrt(sq * (1.0 / d) + LN_EPS)
        for sl in tiles:
            o_ref[:, sl] = (o_ref[:, sl] - mean) * inv * g_ref[:, sl] + b_ref[:, sl]


def _out_proj_layernorm(hf, hr, w_out_bf16, x2, ln_gain, ln_bias, alpha, tm, tn):
    m, d = x2.shape
    kf, kr = hf.shape[1], hr.shape[1]
    assert kf == kr and kf + kr == w_out_bf16.shape[0]
    assert x2.dtype == F32, "the output block doubles as the f32 pre-norm accumulator"
    tm, tn = min(tm, m), min(tn, d)
    n_tiles = d // tn
    return pl.pallas_call(
        functools.partial(_out_kernel, alpha=alpha, n_tiles=n_tiles),
        grid=(m // tm, n_tiles),
        in_specs=[pl.BlockSpec((tm, kf), lambda i, j: (i, 0)),
                  pl.BlockSpec((tm, kr), lambda i, j: (i, 0)),
                  pl.BlockSpec((kf, tn), lambda i, j: (0, j)),
                  pl.BlockSpec((kr, tn), lambda i, j: (1, j)),
                  pl.BlockSpec((tm, tn), lambda i, j: (i, j)),
                  pl.BlockSpec((1, d), lambda i, j: (0, 0)),
                  pl.BlockSpec((1, d), lambda i, j: (0, 0))],
        out_specs=pl.BlockSpec((tm, d), lambda i, j: (i, 0)),
        out_shape=jax.ShapeDtypeStruct((m, d), x2.dtype),
        name="out_proj_layernorm",
        scratch_shapes=[pltpu.VMEM((tm, 1), F32)],
        compiler_params=pltpu.CompilerParams(
            dimension_semantics=("parallel", "arbitrary"), vmem_limit_bytes=VMEM_LIMIT_BYTES),
    )(hf, hr, w_out_bf16, w_out_bf16, x2, ln_gain.reshape(1, d).astype(F32), ln_bias.reshape(1, d).astype(F32))


def kernel(x, w_in, f_bias, mu_shift, w0, w_up, a0, a_up, k_k, k_a, r_k, gn_gain, gn_bias, w_out, ln_gain, ln_bias):
    b, s, d = x.shape
    fox_heads = f_bias.shape[0]
    fw = fox_heads * FOX_HEAD_DIM
    rw = w0.shape[0]
    n_pairs = rw // LANES
    assert r_k.shape[1] == RWKV_HEAD_DIM and rw % LANES == 0
    dl, al = w_up.shape[0], a_up.shape[0]
    mix = fw + rw
    assert w_in.shape[1] == 3 * fw + fox_heads + 3 * rw + dl + al + mix
    depth = 1
    alpha = (2 * depth) ** 0.25

    o_ff = 3 * fw
    o_r = o_ff + fox_heads
    o_wd = o_r + 3 * rw
    o_ad = o_wd + dl
    o_z = o_ad + al

    lora_cols = -(-(dl + al) // 256) * 256
    assert (3 * rw) % lora_cols == 0 and fox_heads <= LANES
    assert o_r + 3 * rw + lora_cols <= w_in.shape[1] and o_ff + LANES <= w_in.shape[1]
    assert all(off % BF16_SUBLANES == 0 for off in (o_ff, o_r, o_z)), "weight row windows must be tile aligned"

    x2 = x.reshape(b * s, d)
    w_t = w_in.T.astype(BF16)
    tn_bf = _pick_tile(math.gcd(fw, mix), 1024)
    tn_f32 = _pick_tile(3 * rw + lora_cols, 1280)
    mu_row = jnp.concatenate([mu_shift.astype(F32), jnp.zeros((lora_cols - dl - al,), F32)]).reshape(1, -1)

    xb = x2.astype(BF16)
    qkvz = _in_proj_fox(xb, w_t, 1024, tn_bf, fw // tn_bf, 3 * fw // tn_bf, o_z, mix // tn_bf,
                        LOG2_E * FOX_HEAD_DIM ** -0.5).reshape(b, s, 3 * fw + mix)
    rkvl, ff = _in_proj_rwkv(xb, w_t, mu_row, 512, tn_f32, o_r, (3 * rw + lora_cols) // tn_f32, o_ff, s)
    rkvl = rkvl.reshape(b, s, 3 * rw + lora_cols)

    bias_row = jnp.zeros((1, LANES), F32).at[0, :fox_heads].set(f_bias.astype(F32))
    c2 = _gate_cumsum(ff.reshape(b, s, LANES), bias_row, 0, 512)
    hf = _fox_attention(qkvz, c2, fox_heads, min(512, s), 0, 2)

    pad_rows = lambda w, off: jnp.zeros((lora_cols, rw), F32).at[off:off + w.shape[0]].set(w.astype(F32)).astype(BF16)
    rows = {}
    for name, val in (("w0", w0), ("a0", a0), ("k_k", k_k), ("k_a", k_a), ("r_k", r_k),
                      ("gn_gain", gn_gain), ("gn_bias", gn_bias)):
        rows[name] = val.astype(F32).reshape(1, rw)
    hr = _rwkv_mix(rkvl, qkvz, rows, pad_rows(w_up, 0), pad_rows(a_up, dl), n_pairs, lora_cols,
                   3 * fw + fw, 512, 2)

    out = _out_proj_layernorm(hf.reshape(b * s, fw), hr.reshape(b * s, rw), w_out.astype(BF16),
                              x2, ln_gain, ln_bias, alpha, 512, 1024)
    return out.reshape(b, s, d)
```

```python
import functools
import math
from typing import NamedTuple

import jax
import jax.numpy as jnp
from jax import lax
from jax.experimental import pallas as pl
from jax.experimental.pallas import tpu as pltpu

F32 = jnp.float32
BF16 = jnp.bfloat16

LANES = 128
BF16_SUBLANES = 16
FOX_HEAD_DIM = 128
RWKV_HEAD_DIM = 64
RWKV_CHUNK = 64
LOG2_E = 1.4426950408889634
EXP_NEG_HALF = 0.6065306597126334
LN_EPS = 1e-5
GN_EPS = 64e-5
KEY_NORM_FLOOR = 1e-12
LAYER_DEPTH = 1
VMEM_LIMIT_BYTES = 56 * 1024 * 1024


class _Tiles(NamedTuple):
    proj_fox_rows: int = 1024
    proj_fox_cols: int = 1024
    proj_rwkv_rows: int = 512
    proj_rwkv_cols: int = 1280
    gate_rows: int = 512
    attn_block: int = 512
    attn_heads: int = 2
    rwkv_frames: int = 512
    rwkv_pairs: int = 2
    out_rows: int = 512
    out_cols: int = 1024


TILES = _Tiles()


def _bdot(a, b, dims=(((1,), (0,)), ((), ()))):
    return lax.dot_general(a.astype(BF16), b.astype(BF16), dims, preferred_element_type=F32)


def _sigmoid(x):
    return 0.5 * jnp.tanh(0.5 * x) + 0.5


def _split3_dot(m_bf16, x):
    hi = x.astype(BF16)
    r1 = x - hi.astype(F32)
    mid = r1.astype(BF16)
    lo = (r1 - mid.astype(F32)).astype(BF16)
    dot = functools.partial(jnp.dot, preferred_element_type=F32)
    return dot(m_bf16, hi) + dot(m_bf16, mid) + dot(m_bf16, lo)


def _split2_dot(m_bf16, x):
    hi = x.astype(BF16)
    lo = (x - hi.astype(F32)).astype(BF16)
    dot = functools.partial(jnp.dot, preferred_element_type=F32)
    return dot(m_bf16, hi) + dot(m_bf16, lo)


def _pick_tile(n, preferred):
    t = min(preferred, n) // LANES * LANES
    while n % t:
        t -= LANES
    return t


_NT = (((1,), (1,)), ((), ()))


def _proj_fox_kernel(a_ref, wt_ref, o_ref, *, n_q_blocks, n_qkv_blocks, q_scale):
    j = pl.program_id(1)
    acc = lax.dot_general(a_ref[...], wt_ref[...], _NT, preferred_element_type=F32)
    factor = jnp.where(j >= n_qkv_blocks, _sigmoid(acc), jnp.where(j < n_q_blocks, q_scale, 1.0))
    o_ref[...] = (acc * factor).astype(o_ref.dtype)


def _proj_rwkv_kernel(a_ref, wt_ref, et_ref, mu_ref, o_ref, e_ref, last_ref, *, blocks_per_seq):
    i, j = pl.program_id(0), pl.program_id(1)

    @pl.when(j == 0)
    def _():
        e_ref[...] = lax.dot_general(a_ref[...], et_ref[...], _NT, preferred_element_type=F32)

    p = lax.dot_general(a_ref[...], wt_ref[...], _NT, preferred_element_type=F32)
    tm = p.shape[0]
    above = jnp.where(i % blocks_per_seq == 0, 0.0, last_ref[j])
    first_row = lax.broadcasted_iota(jnp.int32, (tm, 1), 0) == 0
    prev = jnp.where(first_row, above, pltpu.roll(p, 1, axis=0))
    last_ref[j] = p[tm - 1:tm, :]
    o_ref[...] = p + (prev - p) * mu_ref[...]


def _w_row_map(segments, tn):
    def w_row(j):
        row, first = None, 0
        for nb, row0 in segments:
            here = row0 + (j - first) * tn
            row = here if row is None else jnp.where(j >= first, here, row)
            first += nb
        return pl.multiple_of(row, BF16_SUBLANES)
    return w_row


def _in_proj_fox(a, w_t, tm, tn, n_q_blocks, n_qkv_blocks, z_row0, n_z_blocks, q_scale):
    m, k = a.shape
    tm = _pick_tile(m, tm)
    w_row = _w_row_map([(n_qkv_blocks, 0), (n_z_blocks, z_row0)], tn)
    n_blocks = n_qkv_blocks + n_z_blocks
    return pl.pallas_call(
        functools.partial(_proj_fox_kernel, n_q_blocks=n_q_blocks, n_qkv_blocks=n_qkv_blocks, q_scale=q_scale),
        grid=(m // tm, n_blocks),
        in_specs=[pl.BlockSpec((tm, k), lambda i, j: (i, 0)),
                  pl.BlockSpec((pl.Element(tn), pl.Element(k)), lambda i, j: (w_row(j), 0))],
        out_specs=pl.BlockSpec((tm, tn), lambda i, j: (i, j)),
        out_shape=jax.ShapeDtypeStruct((m, n_blocks * tn), BF16),
        name="in_proj_fox",
        compiler_params=pltpu.CompilerParams(
            dimension_semantics=("parallel", "arbitrary"), vmem_limit_bytes=VMEM_LIMIT_BYTES),
    )(a, w_t)


def _in_proj_rwkv(a, w_t, mu_row, tm, tn, row0, n_blocks, gate_row0, seq_len):
    m, k = a.shape
    tm = _pick_tile(math.gcd(m, seq_len), tm)
    w_row = _w_row_map([(n_blocks, row0)], tn)
    return pl.pallas_call(
        functools.partial(_proj_rwkv_kernel, blocks_per_seq=seq_len // tm),
        grid=(m // tm, n_blocks),
        in_specs=[pl.BlockSpec((tm, k), lambda i, j: (i, 0)),
                  pl.BlockSpec((pl.Element(tn), pl.Element(k)), lambda i, j: (w_row(j), 0)),
                  pl.BlockSpec((pl.Element(LANES), pl.Element(k)), lambda i, j: (gate_row0, 0)),
                  pl.BlockSpec((1, tn), lambda i, j: (0, j))],
        out_specs=[pl.BlockSpec((tm, tn), lambda i, j: (i, j)),
                   pl.BlockSpec((tm, LANES), lambda i, j: (i, 0))],
        out_shape=[jax.ShapeDtypeStruct((m, n_blocks * tn), F32), jax.ShapeDtypeStruct((m, LANES), F32)],
        name="in_proj_rwkv",
        scratch_shapes=[pltpu.VMEM((n_blocks, 1, tn), F32)],
        compiler_params=pltpu.CompilerParams(
            dimension_semantics=("arbitrary", "arbitrary"), vmem_limit_bytes=VMEM_LIMIT_BYTES),
    )(a, w_t, w_t, mu_row)


def _gate_kernel(x_ref, b_ref, o_ref, carry_ref):
    @pl.when(pl.program_id(1) == 0)
    def _():
        carry_ref[...] = jnp.zeros_like(carry_ref)

    t = x_ref.shape[1]
    x = x_ref[0] + b_ref[...]
    logf = jnp.minimum(x, 0.0) - jnp.log1p(jnp.exp(-jnp.abs(x)))
    tri = (lax.broadcasted_iota(jnp.int32, (t, t), 1)
           <= lax.broadcasted_iota(jnp.int32, (t, t), 0)).astype(BF16)
    c = _split3_dot(tri, logf) + carry_ref[0:1, :]
    o_ref[0] = c * LOG2_E
    carry_ref[0:1, :] = c[t - 1:t, :]


def _gate_cumsum(lora3, bias_row, col_block, t):
    b, s, _ = lora3.shape
    t = min(t, s)
    return pl.pallas_call(
        _gate_kernel,
        grid=(b, s // t),
        in_specs=[pl.BlockSpec((1, t, LANES), lambda bi, i: (bi, i, col_block)),
                  pl.BlockSpec((1, LANES), lambda bi, i: (0, 0))],
        out_specs=pl.BlockSpec((1, t, LANES), lambda bi, i: (bi, i, 0)),
        out_shape=jax.ShapeDtypeStruct((b, s, LANES), F32),
        name="fox_gate_cumsum",
        scratch_shapes=[pltpu.VMEM((8, LANES), F32)],
        compiler_params=pltpu.CompilerParams(dimension_semantics=("parallel", "arbitrary")),
    )(lora3, bias_row)


def _fox_kernel(q_ref, k_ref, v_ref, c_ref, z_ref, o_ref, kaug_ref, vt_ref, *unit_refs, tq, c_lane0, group_cols):
    i = pl.program_id(2)
    s_len = k_ref.shape[1]
    dh = FOX_HEAD_DIM
    n_heads = k_ref.shape[2] // dh
    gw = group_cols
    n_groups = tq // gw
    units = [(hh, g) for hh in range(n_heads) for g in range(n_groups)]
    nu = len(units)
    sa, sb, pa, pb, ta, tb, m_refs, al_refs, acc_refs = (unit_refs[kind * nu:(kind + 1) * nu] for kind in range(9))

    @pl.when(i == 0)
    def _():
        lane_r = lax.broadcasted_iota(jnp.int32, (LANES, LANES), 0)
        lane_c = lax.broadcasted_iota(jnp.int32, (LANES, LANES), 1)
        dot = functools.partial(jnp.dot, preferred_element_type=F32)
        for hh in range(n_heads):
            c_lane = c_lane0 + pl.program_id(1) * n_heads + hh
            pick = [jnp.where((lane_r == c_lane) & (lane_c == term), 1.0, 0.0).astype(BF16) for term in range(3)]
            hs = slice(hh * dh, (hh + 1) * dh)
            for jb in range(s_len // tq):
                sl = slice(jb * tq, (jb + 1) * tq)
                c = c_ref[0, sl, :]
                hi = c.astype(BF16)
                r1 = c - hi.astype(F32)
                mid = r1.astype(BF16)
                lo = (r1 - mid.astype(F32)).astype(BF16)
                c_terms = dot(hi, pick[0]) + dot(mid, pick[1]) + dot(lo, pick[2])
                kaug_ref[hh, sl, :dh] = k_ref[0, sl, hs]
                kaug_ref[hh, sl, dh:] = (-c_terms).astype(BF16)
                vt_ref[hh, jb, :dh, :] = v_ref[0, sl, hs].T
                vt_ref[hh, jb, dh:, :] = jnp.ones((vt_ref.shape[2] - dh, tq), BF16)

    ones3 = jnp.where(lax.broadcasted_iota(jnp.int32, (dh, tq), 0) < 3, 1.0, 0.0)
    q_t = [jnp.concatenate([q_ref[0, :, hh * dh:(hh + 1) * dh].astype(F32).T, ones3], axis=0).astype(BF16)
           for hh in range(n_heads)]
    q_u = [q_t[hh][:, g * gw:(g + 1) * gw] for (hh, g) in units]

    def scores(j, s_out, top_out):
        rows = pl.ds(pl.multiple_of(j * tq, tq), tq)
        for u, (hh, _) in enumerate(units):
            s = jnp.dot(kaug_ref[hh, rows, :], q_u[u], preferred_element_type=F32)
            s_out[u][...] = s
            top_out[u][...] = jnp.max(s, axis=0, keepdims=True)

    def accumulate(p, j):
        for u, (hh, _) in enumerate(units):
            acc_refs[u][...] = (al_refs[u][...] * acc_refs[u][...]
                                + jnp.dot(vt_ref[hh, j], p[u], preferred_element_type=F32))

    def softmax_step(s, top):
        p = []
        for u in range(nu):
            m = m_refs[u][...]
            m_new = jnp.maximum(m, top[u])
            p.append(jnp.exp2(s[u] - m_new).astype(BF16))
            al_refs[u][...] = jnp.exp2(m - m_new)
            m_refs[u][...] = m_new
        return p

    def step(j, s_in, s_out, top_in, top_out, p_in, p_out):
        scores(j + 1, s_out, top_out)
        accumulate([p_in[u][...] for u in range(nu)], jnp.maximum(j - 1, 0))
        p = softmax_step([s_in[u][...] for u in range(nu)], [top_in[u][...] for u in range(nu)])
        for u in range(nu):
            p_out[u][...] = p[u]

    scores(0, sa, ta)
    for u in range(nu):
        pa[u][...] = jnp.zeros_like(pa[u])
        m_refs[u][...] = jnp.full(m_refs[u].shape, -jnp.inf, F32)
        al_refs[u][...] = jnp.ones_like(al_refs[u])
        acc_refs[u][...] = jnp.zeros_like(acc_refs[u])

    def pair(t, carry):
        step(2 * t, sa, sb, ta, tb, pa, pb)
        step(2 * t + 1, sb, sa, tb, ta, pb, pa)
        return carry

    lax.fori_loop(0, i // 2, pair, 0)

    def finish(s_last, p_last):
        accumulate([p_last[u][...] for u in range(nu)], jnp.maximum(i - 1, 0))
        key = lax.broadcasted_iota(jnp.int32, (tq, gw), 0)
        qry = lax.broadcasted_iota(jnp.int32, (tq, gw), 1)
        s_diag = [jnp.where(key <= qry + g * gw, s_last[u][...], -jnp.inf) for u, (_, g) in enumerate(units)]
        p = softmax_step(s_diag, [jnp.max(s_u, axis=0, keepdims=True) for s_u in s_diag])
        accumulate(p, i)
        for u, (hh, g) in enumerate(units):
            acc_u = acc_refs[u][...]
            gate = z_ref[0, g * gw:(g + 1) * gw, hh * dh:(hh + 1) * dh].astype(F32)
            out = (acc_u[:dh, :] / acc_u[dh:dh + 1, :]).T
            o_ref[0, g * gw:(g + 1) * gw, hh * dh:(hh + 1) * dh] = (out * gate).astype(o_ref.dtype)

    @pl.when(i % 2 == 1)
    def _():
        step(i - 1, sa, sb, ta, tb, pa, pb)
        finish(sb, pb)

    @pl.when(i % 2 == 0)
    def _():
        finish(sa, pa)


def _fox_attention(qkvz, c2, n_heads, tq, c_lane0, heads_per_step):
    b, s, _ = qkvz.shape
    nq = s // tq
    nh = heads_per_step
    hb = n_heads // nh
    assert hb * nh == n_heads
    w = nh * FOX_HEAD_DIM
    acc_rows = FOX_HEAD_DIM + BF16_SUBLANES
    gw = min(256, tq)
    n_units = nh * (tq // gw)
    return pl.pallas_call(
        functools.partial(_fox_kernel, tq=tq, c_lane0=c_lane0, group_cols=gw),
        grid=(b, hb, nq),
        in_specs=[pl.BlockSpec((1, tq, w), lambda bi, hi, i: (bi, i, hi)),
                  pl.BlockSpec((1, s, w), lambda bi, hi, i: (bi, 0, hb + hi)),
                  pl.BlockSpec((1, s, w), lambda bi, hi, i: (bi, 0, 2 * hb + hi)),
                  pl.BlockSpec((1, s, LANES), lambda bi, hi, i: (bi, 0, 0)),
                  pl.BlockSpec((1, tq, w), lambda bi, hi, i: (bi, i, 3 * hb + hi))],
        out_specs=pl.BlockSpec((1, tq, w), lambda bi, hi, i: (bi, i, hi)),
        out_shape=jax.ShapeDtypeStruct((b, s, n_heads * FOX_HEAD_DIM), BF16),
        name="fox_attention",
        scratch_shapes=[pltpu.VMEM((nh, s, 2 * FOX_HEAD_DIM), BF16), pltpu.VMEM((nh, nq, acc_rows, tq), BF16)]
        + [pltpu.VMEM((tq, gw), F32)] * (2 * n_units)
        + [pltpu.VMEM((tq, gw), BF16)] * (2 * n_units)
        + [pltpu.VMEM((1, gw), F32)] * (4 * n_units)
        + [pltpu.VMEM((acc_rows, gw), F32)] * n_units,
        compiler_params=pltpu.CompilerParams(
            dimension_semantics=("parallel", "parallel", "arbitrary"), vmem_limit_bytes=VMEM_LIMIT_BYTES),
    )(qkvz, qkvz, qkvz, c2, qkvz)


def _stack_heads(x, head0):
    zero = jnp.zeros_like(x)
    return jnp.concatenate([jnp.where(head0, x, zero), jnp.where(head0, zero, x)], axis=0)


def _rwkv_chunks(units, masks):
    head0, strict, diag, blockdiag, eye = masks
    incl = strict | diag
    c = units[0]["r"].shape[0]
    n_units = range(len(units))
    zero = jnp.zeros((c, LANES), F32)
    nt = (((1,), (1,)), ((), ()))
    tn = (((0,), (0,)), ((), ()))

    rt, at, bh, kh, gc, sv, lhs1, rhs1 = [], [], [], [], [], [], [], []
    for u in units:
        lg = u["lg"]
        lgc = lg[c - 1:c, :]
        e_neg = jnp.exp2(-lg)
        e_end = jnp.exp2(lgc - lg)
        rt.append(u["r"] * jnp.exp2(lg))
        at.append(u["av"] * jnp.exp2(lg - u["logw"]))
        bh.append(u["bv"] * e_end)
        kh.append(u["k2"] * e_end)
        gc.append(jnp.exp2(lgc))
        sv.append(_stack_heads(u["v"], head0))
        lhs1.append(jnp.concatenate([at[-1], rt[-1]], axis=0))
        rhs1.append(jnp.concatenate([_stack_heads(u["bv"] * e_neg, head0),
                                     _stack_heads(u["k2"] * e_neg, head0)], axis=0))

    big = [_bdot(lhs1[i], rhs1[i], nt) for i in n_units]
    l_mat = [jnp.where(strict, big[i][:c, :LANES], zero) for i in n_units]
    a_ak = [jnp.where(strict, big[i][:c, LANES:], zero) for i in n_units]
    a_r = [jnp.concatenate([jnp.where(incl, big[i][c:, :LANES], zero),
                            jnp.where(incl, big[i][c:, LANES:], zero)], axis=1) for i in n_units]

    aakv = [_bdot(a_ak[i], sv[i]) for i in n_units]
    n_steps = max(1, (c - 1).bit_length())
    t_inv = [jnp.where(diag, 1.0, 0.0) + l_mat[i] for i in n_units]
    cur = [_bdot(l_mat[i], _stack_heads(l_mat[i], head0)) for i in n_units] if n_steps > 1 else None
    for step in range(1, n_steps):
        if step + 1 < n_steps:
            both = [_bdot(jnp.concatenate([cur[i], t_inv[i]], axis=0), _stack_heads(cur[i], head0)) for i in n_units]
            t_inv = [t_inv[i] + both[i][c:] for i in n_units]
            cur = [both[i][:c] for i in n_units]
        else:
            t_inv = [t_inv[i] + _bdot(t_inv[i], _stack_heads(cur[i], head0)) for i in n_units]
    x = [_bdot(t_inv[i], jnp.concatenate([_stack_heads(at[i], head0), _stack_heads(aakv[i], head0)], axis=1))
         for i in n_units]

    rhs2 = [jnp.concatenate(
        [jnp.concatenate([_stack_heads(x[i][:, :LANES], head0), _stack_heads(x[i][:, LANES:], head0)], axis=1),
         jnp.concatenate([jnp.zeros((2 * c, LANES), F32), sv[i]], axis=1)], axis=0) for i in n_units]
    qy = [_bdot(a_r[i], rhs2[i]) for i in n_units]
    gh = [_bdot(jnp.concatenate([bh[i], kh[i]], axis=0),
                jnp.concatenate([x[i], jnp.concatenate([zero, units[i]["v"]], axis=1)], axis=0), tn)
          for i in n_units]
    out = []
    for i in n_units:
        out.append((rt[i] + qy[i][:, :LANES], qy[i][:, LANES:],
                    jnp.where(blockdiag, gh[i][:, :LANES], 0.0) + eye * gc[i],
                    jnp.where(blockdiag, gh[i][:, LANES:], 0.0)))
    return out


def _rwkv_kernel(r_ref, k_ref, v_ref, lo_ref, z_ref, wup_ref, aup_ref, w0_ref, a0_ref, kk_ref, ka_ref, rk_ref,
                 gg_ref, gb_ref, o_ref, sbd_ref):
    nb, t, w = r_ref.shape
    c = RWKV_CHUNK
    n = RWKV_HEAD_DIM
    n_pairs = w // LANES
    n_chunks = t // c

    @pl.when(pl.program_id(1) == 0)
    def _():
        sbd_ref[...] = jnp.zeros_like(sbd_ref)

    wr = lax.broadcasted_iota(jnp.int32, (w, w), 0)
    wc = lax.broadcasted_iota(jnp.int32, (w, w), 1)
    head_ones = ((wr // n) == (wc // n)).astype(BF16)
    gr = min(t, 2 * c)
    tr = lax.broadcasted_iota(jnp.int32, (gr, gr), 0)
    tc = lax.broadcasted_iota(jnp.int32, (gr, gr), 1)
    chunk_tri = ((tc <= tr) & ((tc // c) == (tr // c))).astype(BF16)

    def chunk_cumsum(x):
        return jnp.concatenate([_split2_dot(chunk_tri, x[g0:g0 + gr]) for g0 in range(0, t, gr)], axis=0)

    def head_sum(x):
        return _bdot(x, head_ones)

    prepped = []
    for bi in range(nb):
        r, k, v, lora = r_ref[bi], k_ref[bi], v_ref[bi], lo_ref[bi]
        dec = _bdot(jnp.tanh(lora), wup_ref[...])
        logw = -(EXP_NEG_HALF * LOG2_E) * _sigmoid(w0_ref[...] + dec)
        a = _sigmoid(a0_ref[...] + _bdot(lora, aup_ref[...]))
        kk = k * kk_ref[...]
        kk = kk * lax.rsqrt(jnp.maximum(head_sum(kk * kk), KEY_NORM_FLOOR ** 2))
        k2 = k * (1.0 + (a - 1.0) * ka_ref[...])
        prepped.append(dict(r=r, k2=k2, v=v, av=-kk, bv=kk * a, logw=logw, lg=chunk_cumsum(logw)))

    lane = lax.broadcasted_iota(jnp.int32, (c, LANES), 1)
    row = lax.broadcasted_iota(jnp.int32, (c, LANES), 0)
    head0 = lane < n
    col = jnp.where(head0, lane, lane - n)
    lane_r = lax.broadcasted_iota(jnp.int32, (LANES, LANES), 0)
    lane_c = lax.broadcasted_iota(jnp.int32, (LANES, LANES), 1)
    masks = (head0, col < row, col == row, (lane_r < n) == (lane_c < n), (lane_r == lane_c).astype(F32))

    seqs = [(bi, pi) for bi in range(nb) for pi in range(n_pairs)]
    units = [{name: arr[ci * c:(ci + 1) * c, pi * LANES:(pi + 1) * LANES] for name, arr in prepped[bi].items()}
             for ci in range(n_chunks) for (bi, pi) in seqs]
    affine = _rwkv_chunks(units, masks)

    sbd = [sbd_ref[si] for si in range(len(seqs))]
    ys = {}
    for ci in range(n_chunks):
        qs = [affine[ci * len(seqs) + si] for si in range(len(seqs))]
        both = [_bdot(jnp.concatenate([qs[si][0], qs[si][2]], axis=0), sbd[si]) for si in range(len(seqs))]
        sbd = [both[si][c:] + qs[si][3] for si in range(len(seqs))]
        for si, seq in enumerate(seqs):
            ys[seq + (ci,)] = both[si][:c] + qs[si][1]
    for si in range(len(seqs)):
        sbd_ref[si] = sbd[si]

    for bi in range(nb):
        y = jnp.concatenate([jnp.concatenate([ys[(bi, pi, ci)] for pi in range(n_pairs)], axis=1)
                             for ci in range(n_chunks)], axis=0)
        p = prepped[bi]
        mu = head_sum(y) * (1.0 / n)
        d = y - mu
        var = head_sum(d * d) * (1.0 / n)
        y = d * lax.rsqrt(var + GN_EPS) * gg_ref[...] + gb_ref[...]
        y = y + head_sum(p["r"] * p["k2"] * rk_ref[...]) * p["v"]
        o_ref[bi] = (y * z_ref[bi].astype(F32)).astype(o_ref.dtype)


def _rwkv_mix(rkvl, qkvz, rows, wup_pad, aup_pad, n_pairs, lora_cols, z_col0, t, pairs_per_step):
    b, s, _ = rkvl.shape
    t = min(t, s)
    w = pairs_per_step * LANES
    p = n_pairs // pairs_per_step
    assert p * pairs_per_step == n_pairs and z_col0 % w == 0
    lora_block = 3 * n_pairs * LANES // lora_cols
    assert lora_block * lora_cols == 3 * n_pairs * LANES
    tok = lambda off: pl.BlockSpec((b, t, w), lambda j, i: (0, i, off + j))
    prow = lambda off: pl.BlockSpec((1, w), lambda j, i: (0, off + j))
    in_specs = [
        tok(0), tok(p), tok(2 * p),
        pl.BlockSpec((b, t, lora_cols), lambda j, i: (0, i, lora_block)),
        tok(z_col0 // w),
        pl.BlockSpec((lora_cols, w), lambda j, i: (0, j)),
        pl.BlockSpec((lora_cols, w), lambda j, i: (0, j)),
    ] + [prow(0)] * 7
    return pl.pallas_call(
        _rwkv_kernel,
        grid=(p, s // t),
        in_specs=in_specs,
        out_specs=pl.BlockSpec((b, t, w), lambda j, i: (0, i, j)),
        out_shape=jax.ShapeDtypeStruct((b, s, n_pairs * LANES), BF16),
        name="rwkv7_mix",
        scratch_shapes=[pltpu.VMEM((b * pairs_per_step, LANES, LANES), F32)],
        compiler_params=pltpu.CompilerParams(
            dimension_semantics=("parallel", "arbitrary"), vmem_limit_bytes=VMEM_LIMIT_BYTES),
    )(rkvl, rkvl, rkvl, rkvl, qkvz, wup_pad, aup_pad,
      rows["w0"], rows["a0"], rows["k_k"], rows["k_a"], rows["r_k"], rows["gn_gain"], rows["gn_bias"])


def _out_kernel(hf_ref, hr_ref, wt_ref, wb_ref, x_ref, g_ref, b_ref, o_ref, sum_ref, *, alpha, n_tiles):
    j = pl.program_id(1)
    tn = x_ref.shape[1]
    out = (jnp.dot(hf_ref[...], wt_ref[...], preferred_element_type=F32)
           + jnp.dot(hr_ref[...], wb_ref[...], preferred_element_type=F32))
    pre = alpha * x_ref[...] + out
    o_ref[:, pl.ds(pl.multiple_of(j * tn, tn), tn)] = pre
    part = jnp.sum(pre, axis=-1, keepdims=True)
    sum_ref[...] = jnp.where(j == 0, part, sum_ref[...] + part)

    @pl.when(j == n_tiles - 1)
    def _():
        d = n_tiles * tn
        tiles = [slice(jj * tn, (jj + 1) * tn) for jj in range(n_tiles)]
        mean = sum_ref[...] * (1.0 / d)
        sq = jnp.sum(jnp.square(o_ref[:, tiles[0]] - mean), axis=-1, keepdims=True)
        for sl in tiles[1:]:
            sq = sq + jnp.sum(jnp.square(o_ref[:, sl] - mean), axis=-1, keepdims=True)
        inv = lax.rsqrt(sq * (1.0 / d) + LN_EPS)
        for sl in tiles:
            o_ref[:, sl] = (o_ref[:, sl] - mean) * inv * g_ref[:, sl] + b_ref[:, sl]


def _out_proj_layernorm(hf, hr, w_out_bf16, x2, ln_gain, ln_bias, alpha, tm, tn):
    m, d = x2.shape
    kf, kr = hf.shape[1], hr.shape[1]
    assert kf == kr and kf + kr == w_out_bf16.shape[0]
    assert x2.dtype == F32, "the output block doubles as the f32 pre-norm accumulator"
    tm, tn = min(tm, m), min(tn, d)
    n_tiles = d // tn
    return pl.pallas_call(
        functools.partial(_out_kernel, alpha=alpha, n_tiles=n_tiles),
        grid=(m // tm, n_tiles),
        in_specs=[pl.BlockSpec((tm, kf), lambda i, j: (i, 0)),
                  pl.BlockSpec((tm, kr), lambda i, j: (i, 0)),
                  pl.BlockSpec((kf, tn), lambda i, j: (0, j)),
                  pl.BlockSpec((kr, tn), lambda i, j: (1, j)),
                  pl.BlockSpec((tm, tn), lambda i, j: (i, j)),
                  pl.BlockSpec((1, d), lambda i, j: (0, 0)),
                  pl.BlockSpec((1, d), lambda i, j: (0, 0))],
        out_specs=pl.BlockSpec((tm, d), lambda i, j: (i, 0)),
        out_shape=jax.ShapeDtypeStruct((m, d), x2.dtype),
        name="out_proj_layernorm",
        scratch_shapes=[pltpu.VMEM((tm, 1), F32)],
        compiler_params=pltpu.CompilerParams(
            dimension_semantics=("parallel", "arbitrary"), vmem_limit_bytes=VMEM_LIMIT_BYTES),
    )(hf, hr, w_out_bf16, w_out_bf16, x2, ln_gain.reshape(1, d).astype(F32), ln_bias.reshape(1, d).astype(F32))


def kernel(x, w_in, f_bias, mu_shift, w0, w_up, a0, a_up, k_k, k_a, r_k, gn_gain, gn_bias, w_out, ln_gain, ln_bias):
    b, s, d = x.shape
    fox_heads = f_bias.shape[0]
    fw = fox_heads * FOX_HEAD_DIM
    rw = w0.shape[0]
    n_pairs = rw // LANES
    assert r_k.shape[1] == RWKV_HEAD_DIM and rw % LANES == 0
    dl, al = w_up.shape[0], a_up.shape[0]
    mix = fw + rw
    assert w_in.shape[1] == 3 * fw + fox_heads + 3 * rw + dl + al + mix
    alpha = (2 * LAYER_DEPTH) ** 0.25

    o_ff = 3 * fw
    o_r = o_ff + fox_heads
    o_wd = o_r + 3 * rw
    o_ad = o_wd + dl
    o_z = o_ad + al

    lora_cols = -(-(dl + al) // 256) * 256
    assert (3 * rw) % lora_cols == 0 and fox_heads <= LANES
    assert o_r + 3 * rw + lora_cols <= w_in.shape[1] and o_ff + LANES <= w_in.shape[1]
    assert all(off % BF16_SUBLANES == 0 for off in (o_ff, o_r, o_z)), "weight row windows must be tile aligned"

    x2 = x.reshape(b * s, d)
    w_t = w_in.T.astype(BF16)
    tn_bf = _pick_tile(math.gcd(fw, mix), TILES.proj_fox_cols)
    tn_f32 = _pick_tile(3 * rw + lora_cols, TILES.proj_rwkv_cols)
    mu_row = jnp.concatenate([mu_shift.astype(F32), jnp.zeros((lora_cols - dl - al,), F32)]).reshape(1, -1)

    xb = x2.astype(BF16)
    qkvz = _in_proj_fox(xb, w_t, TILES.proj_fox_rows, tn_bf, fw // tn_bf, 3 * fw // tn_bf, o_z, mix // tn_bf,
                        LOG2_E * FOX_HEAD_DIM ** -0.5).reshape(b, s, 3 * fw + mix)
    rkvl, ff = _in_proj_rwkv(xb, w_t, mu_row, TILES.proj_rwkv_rows, tn_f32, o_r, (3 * rw + lora_cols) // tn_f32,
                             o_ff, s)
    rkvl = rkvl.reshape(b, s, 3 * rw + lora_cols)

    bias_row = jnp.zeros((1, LANES), F32).at[0, :fox_heads].set(f_bias.astype(F32))
    c2 = _gate_cumsum(ff.reshape(b, s, LANES), bias_row, 0, TILES.gate_rows)
    hf = _fox_attention(qkvz, c2, fox_heads, min(TILES.attn_block, s), 0, TILES.attn_heads)

    pad_rows = lambda w, off: jnp.zeros((lora_cols, rw), F32).at[off:off + w.shape[0]].set(w.astype(F32)).astype(BF16)
    rows = {}
    for name, val in (("w0", w0), ("a0", a0), ("k_k", k_k), ("k_a", k_a), ("r_k", r_k),
                      ("gn_gain", gn_gain), ("gn_bias", gn_bias)):
        rows[name] = val.astype(F32).reshape(1, rw)
    hr = _rwkv_mix(rkvl, qkvz, rows, pad_rows(w_up, 0), pad_rows(a_up, dl), n_pairs, lora_cols,
                   3 * fw + fw, TILES.rwkv_frames, TILES.rwkv_pairs)

    out = _out_proj_layernorm(hf.reshape(b * s, fw), hr.reshape(b * s, rw), w_out.astype(BF16),
                              x2, ln_gain, ln_bias, alpha, TILES.out_rows, TILES.out_cols)
    return out.reshape(b, s, d)
```

```python
import functools
import math
from typing import NamedTuple

import jax
import jax.numpy as jnp
from jax import lax
from jax.experimental import pallas as pl
from jax.experimental.pallas import tpu as pltpu

F32 = jnp.float32
BF16 = jnp.bfloat16

LANES = 128
BF16_SUBLANES = 16
FOX_HEAD_DIM = 128
RWKV_HEAD_DIM = 64
RWKV_CHUNK = 64
LOG2_E = 1.4426950408889634
EXP_NEG_HALF = 0.6065306597126334
LN_EPS = 1e-5
GN_EPS = 64e-5
KEY_NORM_FLOOR = 1e-12
LAYER_DEPTH = 1
VMEM_LIMIT_BYTES = 56 * 1024 * 1024


class _Tiles(NamedTuple):
    proj_fox_rows: int = 1024
    proj_fox_cols: int = 1024
    proj_rwkv_rows: int = 512
    proj_rwkv_cols: int = 1280
    gate_rows: int = 512
    attn_block: int = 512
    attn_heads: int = 2
    rwkv_frames: int = 512
    rwkv_pairs: int = 2
    out_rows: int = 512
    out_cols: int = 1024


TILES = _Tiles()


def _bdot(a, b, dims=(((1,), (0,)), ((), ()))):
    return lax.dot_general(a.astype(BF16), b.astype(BF16), dims, preferred_element_type=F32)


def _sigmoid(x):
    return 0.5 * jnp.tanh(0.5 * x) + 0.5


def _split3_dot(m_bf16, x):
    hi = x.astype(BF16)
    r1 = x - hi.astype(F32)
    mid = r1.astype(BF16)
    lo = (r1 - mid.astype(F32)).astype(BF16)
    dot = functools.partial(jnp.dot, preferred_element_type=F32)
    return dot(m_bf16, hi) + dot(m_bf16, mid) + dot(m_bf16, lo)


def _split2_dot(m_bf16, x):
    hi = x.astype(BF16)
    lo = (x - hi.astype(F32)).astype(BF16)
    dot = functools.partial(jnp.dot, preferred_element_type=F32)
    return dot(m_bf16, hi) + dot(m_bf16, lo)


def _pick_tile(n, preferred):
    t = min(preferred, n) // LANES * LANES
    while n % t:
        t -= LANES
    return t


_NT = (((1,), (1,)), ((), ()))


def _proj_fox_kernel(a_ref, wt_ref, o_ref, *, n_q_blocks, n_qkv_blocks, q_scale):
    j = pl.program_id(1)
    acc = lax.dot_general(a_ref[...], wt_ref[...], _NT, preferred_element_type=F32)
    factor = jnp.where(j >= n_qkv_blocks, _sigmoid(acc), jnp.where(j < n_q_blocks, q_scale, 1.0))
    o_ref[...] = (acc * factor).astype(o_ref.dtype)


def _proj_rwkv_kernel(a_ref, wt_ref, et_ref, mu_ref, o_ref, e_ref, last_ref, *, blocks_per_seq):
    i, j = pl.program_id(0), pl.program_id(1)

    @pl.when(j == 0)
    def _():
        e_ref[...] = lax.dot_general(a_ref[...], et_ref[...], _NT, preferred_element_type=F32)

    p = lax.dot_general(a_ref[...], wt_ref[...], _NT, preferred_element_type=F32)
    tm = p.shape[0]
    above = jnp.where(i % blocks_per_seq == 0, 0.0, last_ref[j])
    first_row = lax.broadcasted_iota(jnp.int32, (tm, 1), 0) == 0
    prev = jnp.where(first_row, above, pltpu.roll(p, 1, axis=0))
    last_ref[j] = p[tm - 1:tm, :]
    o_ref[...] = p + (prev - p) * mu_ref[...]


def _w_row_map(segments, tn):
    def w_row(j):
        row, first = None, 0
        for nb, row0 in segments:
            here = row0 + (j - first) * tn
            row = here if row is None else jnp.where(j >= first, here, row)
            first += nb
        return pl.multiple_of(row, BF16_SUBLANES)
    return w_row


def _in_proj_fox(a, w_t, tm, tn, n_q_blocks, n_qkv_blocks, z_row0, n_z_blocks, q_scale):
    m, k = a.shape
    tm = _pick_tile(m, tm)
    w_row = _w_row_map([(n_qkv_blocks, 0), (n_z_blocks, z_row0)], tn)
    n_blocks = n_qkv_blocks + n_z_blocks
    return pl.pallas_call(
        functools.partial(_proj_fox_kernel, n_q_blocks=n_q_blocks, n_qkv_blocks=n_qkv_blocks, q_scale=q_scale),
        grid=(m // tm, n_blocks),
        in_specs=[pl.BlockSpec((tm, k), lambda i, j: (i, 0)),
                  pl.BlockSpec((pl.Element(tn), pl.Element(k)), lambda i, j: (w_row(j), 0))],
        out_specs=pl.BlockSpec((tm, tn), lambda i, j: (i, j)),
        out_shape=jax.ShapeDtypeStruct((m, n_blocks * tn), BF16),
        name="in_proj_fox",
        compiler_params=pltpu.CompilerParams(
            dimension_semantics=("parallel", "arbitrary"), vmem_limit_bytes=VMEM_LIMIT_BYTES),
    )(a, w_t)


def _in_proj_rwkv(a, w_t, mu_row, tm, tn, row0, n_blocks, gate_row0, seq_len):
    m, k = a.shape
    tm = _pick_tile(math.gcd(m, seq_len), tm)
    w_row = _w_row_map([(n_blocks, row0)], tn)
    return pl.pallas_call(
        functools.partial(_proj_rwkv_kernel, blocks_per_seq=seq_len // tm),
        grid=(m // tm, n_blocks),
        in_specs=[pl.BlockSpec((tm, k), lambda i, j: (i, 0)),
                  pl.BlockSpec((pl.Element(tn), pl.Element(k)), lambda i, j: (w_row(j), 0)),
                  pl.BlockSpec((pl.Element(LANES), pl.Element(k)), lambda i, j: (gate_row0, 0)),
                  pl.BlockSpec((1, tn), lambda i, j: (0, j))],
        out_specs=[pl.BlockSpec((tm, tn), lambda i, j: (i, j)),
                   pl.BlockSpec((tm, LANES), lambda i, j: (i, 0))],
        out_shape=[jax.ShapeDtypeStruct((m, n_blocks * tn), F32), jax.ShapeDtypeStruct((m, LANES), F32)],
        name="in_proj_rwkv",
        scratch_shapes=[pltpu.VMEM((n_blocks, 1, tn), F32)],
        compiler_params=pltpu.CompilerParams(
            dimension_semantics=("arbitrary", "arbitrary"), vmem_limit_bytes=VMEM_LIMIT_BYTES),
    )(a, w_t, w_t, mu_row)


def _gate_kernel(x_ref, b_ref, o_ref, carry_ref):
    @pl.when(pl.program_id(1) == 0)
    def _():
        carry_ref[...] = jnp.zeros_like(carry_ref)

    t = x_ref.shape[1]
    x = x_ref[0] + b_ref[...]
    logf = jnp.minimum(x, 0.0) - jnp.log1p(jnp.exp(-jnp.abs(x)))
    tri = (lax.broadcasted_iota(jnp.int32, (t, t), 1)
           <= lax.broadcasted_iota(jnp.int32, (t, t), 0)).astype(BF16)
    c = _split3_dot(tri, logf) + carry_ref[0:1, :]
    o_ref[0] = c * LOG2_E
    carry_ref[0:1, :] = c[t - 1:t, :]


def _gate_cumsum(lora3, bias_row, col_block, t):
    b, s, _ = lora3.shape
    t = min(t, s)
    return pl.pallas_call(
        _gate_kernel,
        grid=(b, s // t),
        in_specs=[pl.BlockSpec((1, t, LANES), lambda bi, i: (bi, i, col_block)),
                  pl.BlockSpec((1, LANES), lambda bi, i: (0, 0))],
        out_specs=pl.BlockSpec((1, t, LANES), lambda bi, i: (bi, i, 0)),
        out_shape=jax.ShapeDtypeStruct((b, s, LANES), F32),
        name="fox_gate_cumsum",
        scratch_shapes=[pltpu.VMEM((8, LANES), F32)],
        compiler_params=pltpu.CompilerParams(dimension_semantics=("parallel", "arbitrary")),
    )(lora3, bias_row)


def _fox_kernel(q_ref, k_ref, v_ref, c_ref, z_ref, o_ref, kaug_ref, vt_ref, *unit_refs, tq, c_lane0, group_cols):
    i = pl.program_id(2)
    s_len = k_ref.shape[1]
    dh = FOX_HEAD_DIM
    n_heads = k_ref.shape[2] // dh
    gw = group_cols
    n_groups = tq // gw
    units = [(hh, g) for hh in range(n_heads) for g in range(n_groups)]
    nu = len(units)
    pa, pb, m_refs, al_refs, acc_refs = (unit_refs[kind * nu:(kind + 1) * nu] for kind in range(5))

    @pl.when(i == 0)
    def _():
        lane_r = lax.broadcasted_iota(jnp.int32, (LANES, LANES), 0)
        lane_c = lax.broadcasted_iota(jnp.int32, (LANES, LANES), 1)
        dot = functools.partial(jnp.dot, preferred_element_type=F32)
        for hh in range(n_heads):
            c_lane = c_lane0 + pl.program_id(1) * n_heads + hh
            pick = [jnp.where((lane_r == c_lane) & (lane_c == term), 1.0, 0.0).astype(BF16) for term in range(3)]
            hs = slice(hh * dh, (hh + 1) * dh)
            for jb in range(s_len // tq):
                sl = slice(jb * tq, (jb + 1) * tq)
                c = c_ref[0, sl, :]
                hi = c.astype(BF16)
                r1 = c - hi.astype(F32)
                mid = r1.astype(BF16)
                lo = (r1 - mid.astype(F32)).astype(BF16)
                c_terms = dot(hi, pick[0]) + dot(mid, pick[1]) + dot(lo, pick[2])
                kaug_ref[hh, sl, :dh] = k_ref[0, sl, hs]
                kaug_ref[hh, sl, dh:] = (-c_terms).astype(BF16)
                vt_ref[hh, jb, :dh, :] = v_ref[0, sl, hs].T
                vt_ref[hh, jb, dh:, :] = jnp.ones((vt_ref.shape[2] - dh, tq), BF16)

    ones3 = jnp.where(lax.broadcasted_iota(jnp.int32, (dh, tq), 0) < 3, 1.0, 0.0)
    q_t = [jnp.concatenate([q_ref[0, :, hh * dh:(hh + 1) * dh].astype(F32).T, ones3], axis=0).astype(BF16)
           for hh in range(n_heads)]
    q_u = [q_t[hh][:, g * gw:(g + 1) * gw] for (hh, g) in units]

    def accumulate(p, j, alpha):
        for u, (hh, _) in enumerate(units):
            acc_refs[u][...] = alpha[u] * acc_refs[u][...] + jnp.dot(vt_ref[hh, j], p[u], preferred_element_type=F32)

    def softmax_block(j, diagonal):
        rows = pl.ds(pl.multiple_of(j * tq, tq), tq)
        p = []
        for u, (hh, g) in enumerate(units):
            s = jnp.dot(kaug_ref[hh, rows, :], q_u[u], preferred_element_type=F32)
            if diagonal:
                key = lax.broadcasted_iota(jnp.int32, (tq, gw), 0)
                qry = lax.broadcasted_iota(jnp.int32, (tq, gw), 1)
                s = jnp.where(key <= qry + g * gw, s, -jnp.inf)
            m = m_refs[u][...]
            m_new = jnp.maximum(m, jnp.max(s, axis=0, keepdims=True))
            p.append(jnp.exp2(s - m_new).astype(BF16))
            al_refs[u][...] = jnp.exp2(m - m_new)
            m_refs[u][...] = m_new
        return p

    def step(j, p_in, p_out):
        p_prev = [p_in[u][...] for u in range(nu)]
        al_prev = [al_refs[u][...] for u in range(nu)]
        p = softmax_block(j, False)
        for u in range(nu):
            p_out[u][...] = p[u]
        accumulate(p_prev, jnp.maximum(j - 1, 0), al_prev)

    for u in range(nu):
        pa[u][...] = jnp.zeros_like(pa[u])
        m_refs[u][...] = jnp.full(m_refs[u].shape, -jnp.inf, F32)
        al_refs[u][...] = jnp.ones_like(al_refs[u])
        acc_refs[u][...] = jnp.zeros_like(acc_refs[u])

    def pair(t, carry):
        step(2 * t, pa, pb)
        step(2 * t + 1, pb, pa)
        return carry

    lax.fori_loop(0, i // 2, pair, 0)

    def finish(p_last):
        accumulate([p_last[u][...] for u in range(nu)], jnp.maximum(i - 1, 0), [al_refs[u][...] for u in range(nu)])
        p = softmax_block(i, True)
        accumulate(p, i, [al_refs[u][...] for u in range(nu)])
        for u, (hh, g) in enumerate(units):
            acc_u = acc_refs[u][...]
            gate = z_ref[0, g * gw:(g + 1) * gw, hh * dh:(hh + 1) * dh].astype(F32)
            out = (acc_u[:dh, :] / acc_u[dh:dh + 1, :]).T
            o_ref[0, g * gw:(g + 1) * gw, hh * dh:(hh + 1) * dh] = (out * gate).astype(o_ref.dtype)

    @pl.when(i % 2 == 1)
    def _():
        step(i - 1, pa, pb)
        finish(pb)

    @pl.when(i % 2 == 0)
    def _():
        finish(pa)


def _fox_attention(qkvz, c2, n_heads, tq, c_lane0, heads_per_step):
    b, s, _ = qkvz.shape
    nq = s // tq
    nh = heads_per_step
    hb = n_heads // nh
    assert hb * nh == n_heads
    w = nh * FOX_HEAD_DIM
    acc_rows = FOX_HEAD_DIM + BF16_SUBLANES
    gw = min(256, tq)
    n_units = nh * (tq // gw)
    return pl.pallas_call(
        functools.partial(_fox_kernel, tq=tq, c_lane0=c_lane0, group_cols=gw),
        grid=(b, hb, nq),
        in_specs=[pl.BlockSpec((1, tq, w), lambda bi, hi, i: (bi, i, hi)),
                  pl.BlockSpec((1, s, w), lambda bi, hi, i: (bi, 0, hb + hi)),
                  pl.BlockSpec((1, s, w), lambda bi, hi, i: (bi, 0, 2 * hb + hi)),
                  pl.BlockSpec((1, s, LANES), lambda bi, hi, i: (bi, 0, 0)),
                  pl.BlockSpec((1, tq, w), lambda bi, hi, i: (bi, i, 3 * hb + hi))],
        out_specs=pl.BlockSpec((1, tq, w), lambda bi, hi, i: (bi, i, hi)),
        out_shape=jax.ShapeDtypeStruct((b, s, n_heads * FOX_HEAD_DIM), BF16),
        name="fox_attention",
        scratch_shapes=[pltpu.VMEM((nh, s, 2 * FOX_HEAD_DIM), BF16), pltpu.VMEM((nh, nq, acc_rows, tq), BF16)]
        + [pltpu.VMEM((tq, gw), BF16)] * (2 * n_units)
        + [pltpu.VMEM((1, gw), F32)] * (2 * n_units)
        + [pltpu.VMEM((acc_rows, gw), F32)] * n_units,
        compiler_params=pltpu.CompilerParams(
            dimension_semantics=("parallel", "parallel", "arbitrary"), vmem_limit_bytes=VMEM_LIMIT_BYTES),
    )(qkvz, qkvz, qkvz, c2, qkvz)


def _stack_heads(x, head0):
    zero = jnp.zeros_like(x)
    return jnp.concatenate([jnp.where(head0, x, zero), jnp.where(head0, zero, x)], axis=0)


def _rwkv_chunks(units, masks):
    head0, strict, diag, blockdiag, eye = masks
    incl = strict | diag
    c = units[0]["r"].shape[0]
    n_units = range(len(units))
    zero = jnp.zeros((c, LANES), F32)
    nt = (((1,), (1,)), ((), ()))
    tn = (((0,), (0,)), ((), ()))

    rt, at, bh, kh, gc, sv, lhs1, rhs1 = [], [], [], [], [], [], [], []
    for u in units:
        lg = u["lg"]
        lgc = lg[c - 1:c, :]
        e_neg = jnp.exp2(-lg)
        e_end = jnp.exp2(lgc - lg)
        rt.append(u["r"] * jnp.exp2(lg))
        at.append(u["av"] * jnp.exp2(lg - u["logw"]))
        bh.append(u["bv"] * e_end)
        kh.append(u["k2"] * e_end)
        gc.append(jnp.exp2(lgc))
        sv.append(_stack_heads(u["v"], head0))
        lhs1.append(jnp.concatenate([at[-1], rt[-1]], axis=0))
        rhs1.append(jnp.concatenate([_stack_heads(u["bv"] * e_neg, head0),
                                     _stack_heads(u["k2"] * e_neg, head0)], axis=0))

    big = [_bdot(lhs1[i], rhs1[i], nt) for i in n_units]
    l_mat = [jnp.where(strict, big[i][:c, :LANES], zero) for i in n_units]
    a_ak = [jnp.where(strict, big[i][:c, LANES:], zero) for i in n_units]
    a_r = [jnp.concatenate([jnp.where(incl, big[i][c:, :LANES], zero),
                            jnp.where(incl, big[i][c:, LANES:], zero)], axis=1) for i in n_units]

    aakv = [_bdot(a_ak[i], sv[i]) for i in n_units]
    n_steps = max(1, (c - 1).bit_length())
    t_inv = [jnp.where(diag, 1.0, 0.0) + l_mat[i] for i in n_units]
    cur = [_bdot(l_mat[i], _stack_heads(l_mat[i], head0)) for i in n_units] if n_steps > 1 else None
    for step in range(1, n_steps):
        if step + 1 < n_steps:
            both = [_bdot(jnp.concatenate([cur[i], t_inv[i]], axis=0), _stack_heads(cur[i], head0)) for i in n_units]
            t_inv = [t_inv[i] + both[i][c:] for i in n_units]
            cur = [both[i][:c] for i in n_units]
        else:
            t_inv = [t_inv[i] + _bdot(t_inv[i], _stack_heads(cur[i], head0)) for i in n_units]
    x = [_bdot(t_inv[i], jnp.concatenate([_stack_heads(at[i], head0), _stack_heads(aakv[i], head0)], axis=1))
         for i in n_units]

    rhs2 = [jnp.concatenate(
        [jnp.concatenate([_stack_heads(x[i][:, :LANES], head0), _stack_heads(x[i][:, LANES:], head0)], axis=1),
         jnp.concatenate([jnp.zeros((2 * c, LANES), F32), sv[i]], axis=1)], axis=0) for i in n_units]
    qy = [_bdot(a_r[i], rhs2[i]) for i in n_units]
    gh = [_bdot(jnp.concatenate([bh[i], kh[i]], axis=0),
                jnp.concatenate([x[i], jnp.concatenate([zero, units[i]["v"]], axis=1)], axis=0), tn)
          for i in n_units]
    out = []
    for i in n_units:
        out.append((rt[i] + qy[i][:, :LANES], qy[i][:, LANES:],
                    jnp.where(blockdiag, gh[i][:, :LANES], 0.0) + eye * gc[i],
                    jnp.where(blockdiag, gh[i][:, LANES:], 0.0)))
    return out


def _rwkv_kernel(r_ref, k_ref, v_ref, lo_ref, z_ref, wup_ref, aup_ref, w0_ref, a0_ref, kk_ref, ka_ref, rk_ref,
                 gg_ref, gb_ref, o_ref, sbd_ref):
    nb, t, w = r_ref.shape
    c = RWKV_CHUNK
    n = RWKV_HEAD_DIM
    n_pairs = w // LANES
    n_chunks = t // c

    @pl.when(pl.program_id(1) == 0)
    def _():
        sbd_ref[...] = jnp.zeros_like(sbd_ref)

    wr = lax.broadcasted_iota(jnp.int32, (w, w), 0)
    wc = lax.broadcasted_iota(jnp.int32, (w, w), 1)
    head_ones = ((wr // n) == (wc // n)).astype(BF16)
    gr = min(t, 2 * c)
    tr = lax.broadcasted_iota(jnp.int32, (gr, gr), 0)
    tc = lax.broadcasted_iota(jnp.int32, (gr, gr), 1)
    chunk_tri = ((tc <= tr) & ((tc // c) == (tr // c))).astype(BF16)

    def chunk_cumsum(x):
        return jnp.concatenate([_split2_dot(chunk_tri, x[g0:g0 + gr]) for g0 in range(0, t, gr)], axis=0)

    def head_sum(x):
        return _bdot(x, head_ones)

    prepped = []
    for bi in range(nb):
        r, k, v, lora = r_ref[bi], k_ref[bi], v_ref[bi], lo_ref[bi]
        dec = _bdot(jnp.tanh(lora), wup_ref[...])
        logw = -(EXP_NEG_HALF * LOG2_E) * _sigmoid(w0_ref[...] + dec)
        a = _sigmoid(a0_ref[...] + _bdot(lora, aup_ref[...]))
        kk = k * kk_ref[...]
        kk = kk * lax.rsqrt(jnp.maximum(head_sum(kk * kk), KEY_NORM_FLOOR ** 2))
        k2 = k * (1.0 + (a - 1.0) * ka_ref[...])
        prepped.append(dict(r=r, k2=k2, v=v, av=-kk, bv=kk * a, logw=logw, lg=chunk_cumsum(logw)))

    lane = lax.broadcasted_iota(jnp.int32, (c, LANES), 1)
    row = lax.broadcasted_iota(jnp.int32, (c, LANES), 0)
    head0 = lane < n
    col = jnp.where(head0, lane, lane - n)
    lane_r = lax.broadcasted_iota(jnp.int32, (LANES, LANES), 0)
    lane_c = lax.broadcasted_iota(jnp.int32, (LANES, LANES), 1)
    masks = (head0, col < row, col == row, (lane_r < n) == (lane_c < n), (lane_r == lane_c).astype(F32))

    seqs = [(bi, pi) for bi in range(nb) for pi in range(n_pairs)]
    units = [{name: arr[ci * c:(ci + 1) * c, pi * LANES:(pi + 1) * LANES] for name, arr in prepped[bi].items()}
             for ci in range(n_chunks) for (bi, pi) in seqs]
    affine = _rwkv_chunks(units, masks)

    sbd = [sbd_ref[si] for si in range(len(seqs))]
    ys = {}
    for ci in range(n_chunks):
        qs = [affine[ci * len(seqs) + si] for si in range(len(seqs))]
        both = [_bdot(jnp.concatenate([qs[si][0], qs[si][2]], axis=0), sbd[si]) for si in range(len(seqs))]
        sbd = [both[si][c:] + qs[si][3] for si in range(len(seqs))]
        for si, seq in enumerate(seqs):
            ys[seq + (ci,)] = both[si][:c] + qs[si][1]
    for si in range(len(seqs)):
        sbd_ref[si] = sbd[si]

    for bi in range(nb):
        y = jnp.concatenate([jnp.concatenate([ys[(bi, pi, ci)] for pi in range(n_pairs)], axis=1)
                             for ci in range(n_chunks)], axis=0)
        p = prepped[bi]
        mu = head_sum(y) * (1.0 / n)
        d = y - mu
        var = head_sum(d * d) * (1.0 / n)
        y = d * lax.rsqrt(var + GN_EPS) * gg_ref[...] + gb_ref[...]
        y = y + head_sum(p["r"] * p["k2"] * rk_ref[...]) * p["v"]
        o_ref[bi] = (y * z_ref[bi].astype(F32)).astype(o_ref.dtype)


def _rwkv_mix(rkvl, qkvz, rows, wup_pad, aup_pad, n_pairs, lora_cols, z_col0, t, pairs_per_step):
    b, s, _ = rkvl.shape
    t = min(t, s)
    w = pairs_per_step * LANES
    p = n_pairs // pairs_per_step
    assert p * pairs_per_step == n_pairs and z_col0 % w == 0
    lora_block = 3 * n_pairs * LANES // lora_cols
    assert lora_block * lora_cols == 3 * n_pairs * LANES
    tok = lambda off: pl.BlockSpec((b, t, w), lambda j, i: (0, i, off + j))
    prow = lambda off: pl.BlockSpec((1, w), lambda j, i: (0, off + j))
    in_specs = [
        tok(0), tok(p), tok(2 * p),
        pl.BlockSpec((b, t, lora_cols), lambda j, i: (0, i, lora_block)),
        tok(z_col0 // w),
        pl.BlockSpec((lora_cols, w), lambda j, i: (0, j)),
        pl.BlockSpec((lora_cols, w), lambda j, i: (0, j)),
    ] + [prow(0)] * 7
    return pl.pallas_call(
        _rwkv_kernel,
        grid=(p, s // t),
        in_specs=in_specs,
        out_specs=pl.BlockSpec((b, t, w), lambda j, i: (0, i, j)),
        out_shape=jax.ShapeDtypeStruct((b, s, n_pairs * LANES), BF16),
        name="rwkv7_mix",
        scratch_shapes=[pltpu.VMEM((b * pairs_per_step, LANES, LANES), F32)],
        compiler_params=pltpu.CompilerParams(
            dimension_semantics=("parallel", "arbitrary"), vmem_limit_bytes=VMEM_LIMIT_BYTES),
    )(rkvl, rkvl, rkvl, rkvl, qkvz, wup_pad, aup_pad,
      rows["w0"], rows["a0"], rows["k_k"], rows["k_a"], rows["r_k"], rows["gn_gain"], rows["gn_bias"])


def _out_kernel(hf_ref, hr_ref, wt_ref, wb_ref, x_ref, g_ref, b_ref, o_ref, sum_ref, *, alpha, n_tiles):
    j = pl.program_id(1)
    tn = x_ref.shape[1]
    out = (jnp.dot(hf_ref[...], wt_ref[...], preferred_element_type=F32)
           + jnp.dot(hr_ref[...], wb_ref[...], preferred_element_type=F32))
    pre = alpha * x_ref[...] + out
    o_ref[:, pl.ds(pl.multiple_of(j * tn, tn), tn)] = pre
    part = jnp.sum(pre, axis=-1, keepdims=True)
    sum_ref[...] = jnp.where(j == 0, part, sum_ref[...] + part)

    @pl.when(j == n_tiles - 1)
    def _():
        d = n_tiles * tn
        tiles = [slice(jj * tn, (jj + 1) * tn) for jj in range(n_tiles)]
        mean = sum_ref[...] * (1.0 / d)
        sq = jnp.sum(jnp.square(o_ref[:, tiles[0]] - mean), axis=-1, keepdims=True)
        for sl in tiles[1:]:
            sq = sq + jnp.sum(jnp.square(o_ref[:, sl] - mean), axis=-1, keepdims=True)
        inv = lax.rsqrt(sq * (1.0 / d) + LN_EPS)
        for sl in tiles:
            o_ref[:, sl] = (o_ref[:, sl] - mean) * inv * g_ref[:, sl] + b_ref[:, sl]


def _out_proj_layernorm(hf, hr, w_out_bf16, x2, ln_gain, ln_bias, alpha, tm, tn):
    m, d = x2.shape
    kf, kr = hf.shape[1], hr.shape[1]
    assert kf == kr and kf + kr == w_out_bf16.shape[0]
    assert x2.dtype == F32, "the output block doubles as the f32 pre-norm accumulator"
    tm, tn = min(tm, m), min(tn, d)
    n_tiles = d // tn
    return pl.pallas_call(
        functools.partial(_out_kernel, alpha=alpha, n_tiles=n_tiles),
        grid=(m // tm, n_tiles),
        in_specs=[pl.BlockSpec((tm, kf), lambda i, j: (i, 0)),
                  pl.BlockSpec((tm, kr), lambda i, j: (i, 0)),
                  pl.BlockSpec((kf, tn), lambda i, j: (0, j)),
                  pl.BlockSpec((kr, tn), lambda i, j: (1, j)),
                  pl.BlockSpec((tm, tn), lambda i, j: (i, j)),
                  pl.BlockSpec((1, d), lambda i, j: (0, 0)),
                  pl.BlockSpec((1, d), lambda i, j: (0, 0))],
        out_specs=pl.BlockSpec((tm, d), lambda i, j: (i, 0)),
        out_shape=jax.ShapeDtypeStruct((m, d), x2.dtype),
        name="out_proj_layernorm",
        scratch_shapes=[pltpu.VMEM((tm, 1), F32)],
        compiler_params=pltpu.CompilerParams(
            dimension_semantics=("parallel", "arbitrary"), vmem_limit_bytes=VMEM_LIMIT_BYTES),
    )(hf, hr, w_out_bf16, w_out_bf16, x2, ln_gain.reshape(1, d).astype(F32), ln_bias.reshape(1, d).astype(F32))


def kernel(x, w_in, f_bias, mu_shift, w0, w_up, a0, a_up, k_k, k_a, r_k, gn_gain, gn_bias, w_out, ln_gain, ln_bias):
    b, s, d = x.shape
    fox_heads = f_bias.shape[0]
    fw = fox_heads * FOX_HEAD_DIM
    rw = w0.shape[0]
    n_pairs = rw // LANES
    assert r_k.shape[1] == RWKV_HEAD_DIM and rw % LANES == 0
    dl, al = w_up.shape[0], a_up.shape[0]
    mix = fw + rw
    assert w_in.shape[1] == 3 * fw + fox_heads + 3 * rw + dl + al + mix
    alpha = (2 * LAYER_DEPTH) ** 0.25

    o_ff = 3 * fw
    o_r = o_ff + fox_heads
    o_wd = o_r + 3 * rw
    o_ad = o_wd + dl
    o_z = o_ad + al

    lora_cols = -(-(dl + al) // 256) * 256
    assert (3 * rw) % lora_cols == 0 and fox_heads <= LANES
    assert o_r + 3 * rw + lora_cols <= w_in.shape[1] and o_ff + LANES <= w_in.shape[1]
    assert all(off % BF16_SUBLANES == 0 for off in (o_ff, o_r, o_z)), "weight row windows must be tile aligned"

    x2 = x.reshape(b * s, d)
    w_t = w_in.T.astype(BF16)
    tn_bf = _pick_tile(math.gcd(fw, mix), TILES.proj_fox_cols)
    tn_f32 = _pick_tile(3 * rw + lora_cols, TILES.proj_rwkv_cols)
    mu_row = jnp.concatenate([mu_shift.astype(F32), jnp.zeros((lora_cols - dl - al,), F32)]).reshape(1, -1)

    xb = x2.astype(BF16)
    qkvz = _in_proj_fox(xb, w_t, TILES.proj_fox_rows, tn_bf, fw // tn_bf, 3 * fw // tn_bf, o_z, mix // tn_bf,
                        LOG2_E * FOX_HEAD_DIM ** -0.5).reshape(b, s, 3 * fw + mix)
    rkvl, ff = _in_proj_rwkv(xb, w_t, mu_row, TILES.proj_rwkv_rows, tn_f32, o_r, (3 * rw + lora_cols) // tn_f32,
                             o_ff, s)
    rkvl = rkvl.reshape(b, s, 3 * rw + lora_cols)

    bias_row = jnp.zeros((1, LANES), F32).at[0, :fox_heads].set(f_bias.astype(F32))
    c2 = _gate_cumsum(ff.reshape(b, s, LANES), bias_row, 0, TILES.gate_rows)
    hf = _fox_attention(qkvz, c2, fox_heads, min(TILES.attn_block, s), 0, TILES.attn_heads)

    pad_rows = lambda w, off: jnp.zeros((lora_cols, rw), F32).at[off:off + w.shape[0]].set(w.astype(F32)).astype(BF16)
    rows = {}
    for name, val in (("w0", w0), ("a0", a0), ("k_k", k_k), ("k_a", k_a), ("r_k", r_k),
                      ("gn_gain", gn_gain), ("gn_bias", gn_bias)):
        rows[name] = val.astype(F32).reshape(1, rw)
    hr = _rwkv_mix(rkvl, qkvz, rows, pad_rows(w_up, 0), pad_rows(a_up, dl), n_pairs, lora_cols,
                   3 * fw + fw, TILES.rwkv_frames, TILES.rwkv_pairs)

    out = _out_proj_layernorm(hf.reshape(b * s, fw), hr.reshape(b * s, rw), w_out.astype(BF16),
                              x2, ln_gain, ln_bias, alpha, TILES.out_rows, TILES.out_cols)
    return out.reshape(b, s, d)
```
